```python
import jax, jax.numpy as jnp
from jax import lax
import numpy as np

D_MODEL = 1024
BATCH = 8
SEQ = 4096
DEPTH = 1
DEC_BATCH = 1
DEC_SEQ = 16384
PAST_LEN = 128

N_META = 16
GRID_W = 64
HEAD_DIM = 64
NA_HEADS = 8
NA_WIN_H = 8
NA_WIN_W = 16
GQA_HEADS = 8
GQA_KV_HEADS = 2
GQA_GROUP = GQA_HEADS // GQA_KV_HEADS
Q_BLOCK = 128
W_A = NA_HEADS * HEAD_DIM
W_B = GQA_HEADS * HEAD_DIM
W_KV = GQA_KV_HEADS * HEAD_DIM
MIX_WIDTH = W_A + W_B
IN_COLS = 3 * W_A + W_B + 2 * W_KV
IN_SPLITS = (W_A, 2 * W_A, 3 * W_A, 3 * W_A + W_B, 3 * W_A + W_B + W_KV)
D_FF = 2816
ROPE_THETA = 10000.0
ROPE_AXIS_DIM = HEAD_DIM // 2
EPS = 1e-6

kernel_name = 'hybrid_natten_axial_gqa_encoder'


def _rmsnorm(x, g):
    xf = x.astype(jnp.float32)
    xf = xf * lax.rsqrt(jnp.mean(xf * xf, axis=-1, keepdims=True) + EPS)
    return (xf * g.astype(jnp.float32)).astype(x.dtype)


def _swiglu(x, w_gate, w_up, w_down):
    return (jax.nn.silu(x @ w_gate) * (x @ w_up)) @ w_down


def _half_ffn(x, norm_pre, w_gate, w_up, w_down, norm_post):
    return x + 0.5 * _rmsnorm(_swiglu(_rmsnorm(x, norm_pre), w_gate, w_up, w_down), norm_post)


def _neighbourhood_attention(q, k, v, rel_bias, meta_bias):
    b, l, h, d = q.shape
    rows = (l - N_META) // GRID_W
    kh = min(NA_WIN_H, rows)
    kw = NA_WIN_W
    scale = HEAD_DIM ** -0.5
    qm, km, vm = q[:, :N_META], k[:, :N_META], v[:, :N_META]
    qg = q[:, N_META:].reshape(b, rows, GRID_W, h, d)
    kg = k[:, N_META:].reshape(b, rows, GRID_W, h, d)
    vg = v[:, N_META:].reshape(b, rows, GRID_W, h, d)
    mb = meta_bias.astype(jnp.float32)[None, :, None, :]
    s_mm = jnp.einsum('bqhd,bkhd->bhqk', qm, km).astype(jnp.float32) * scale + mb
    p_mm = jax.nn.softmax(s_mm, axis=-1).astype(v.dtype)
    y_meta = jnp.einsum('bhqk,bkhd->bqhd', p_mm, vm)
    col = np.arange(GRID_W)
    col_start = np.clip(col - kw // 2, 0, GRID_W - kw)
    col_idx = col_start[:, None] + np.arange(kw)[None, :]
    col_bias_idx = col_idx - col[:, None] + (NA_WIN_W - 1)

    def row_block(r):
        r0 = jnp.clip(r - kh // 2, 0, rows - kh)
        q_r = lax.dynamic_index_in_dim(qg, r, axis=1, keepdims=False)
        k_win = lax.dynamic_slice_in_dim(kg, r0, kh, axis=1)[:, :, col_idx]
        v_win = lax.dynamic_slice_in_dim(vg, r0, kh, axis=1)[:, :, col_idx]
        row_bias_idx = r0 + jnp.arange(kh) - r + (NA_WIN_H - 1)
        bias = rel_bias[:, row_bias_idx][:, :, col_bias_idx]
        s_win = jnp.einsum('bqhd,brqkhd->bhqrk', q_r, k_win).astype(jnp.float32) * scale
        s_win = s_win + jnp.transpose(bias, (0, 2, 1, 3))[None].astype(jnp.float32)
        s_meta = jnp.einsum('bqhd,bkhd->bhqk', q_r, km).astype(jnp.float32) * scale + mb
        s = jnp.concatenate([s_win.reshape(b, h, GRID_W, kh * kw), s_meta], axis=-1)
        p = jax.nn.softmax(s, axis=-1).astype(v.dtype)
        p_win = p[..., :kh * kw].reshape(b, h, GRID_W, kh, kw)
        return (jnp.einsum('bhqrk,brqkhd->bqhd', p_win, v_win)
                + jnp.einsum('bhqk,bkhd->bqhd', p[..., kh * kw:], vm))

    y_grid = lax.map(row_block, jnp.arange(rows))
    y_grid = jnp.transpose(y_grid, (1, 0, 2, 3, 4)).reshape(b, rows * GRID_W, h * d)
    return jnp.concatenate([y_meta.reshape(b, N_META, h * d), y_grid], axis=1)


def _rotate_half(x, ang):
    n = ang.shape[-1]
    cos = jnp.cos(ang)[:, None, :].astype(x.dtype)
    sin = jnp.sin(ang)[:, None, :].astype(x.dtype)
    x1, x2 = x[..., :n], x[..., n:]
    return jnp.concatenate([x1 * cos - x2 * sin, x1 * sin + x2 * cos], axis=-1)


def _axial_rope(x, ang_row, ang_col):
    half = HEAD_DIM // 2
    return jnp.concatenate([_rotate_half(x[..., :half], ang_row),
                            _rotate_half(x[..., half:], ang_col)], axis=-1)


def _axial_angles(n_tokens):
    t = jnp.arange(n_tokens)
    pos_row = jnp.concatenate([-jnp.ones((N_META,), jnp.float32), (t // GRID_W).astype(jnp.float32)])
    pos_col = jnp.concatenate([jnp.arange(N_META, dtype=jnp.float32), (t % GRID_W).astype(jnp.float32)])
    inv_freq = jnp.asarray(ROPE_THETA ** (-np.arange(0, ROPE_AXIS_DIM, 2) / ROPE_AXIS_DIM), jnp.float32)
    return pos_row[:, None] * inv_freq[None, :], pos_col[:, None] * inv_freq[None, :]


def _axial_gqa(q, k, v, q_norm, k_norm):
    b, l, _, d = q.shape
    s = l - N_META
    ang_row, ang_col = _axial_angles(s)
    q = _axial_rope(_rmsnorm(q, q_norm), ang_row, ang_col) * (HEAD_DIM ** -0.5)
    k = _axial_rope(_rmsnorm(k, k_norm), ang_row, ang_col)
    q = q.reshape(b, l, GQA_KV_HEADS, GQA_GROUP, d)

    def attend(qb):
        sc = jnp.einsum('bqngd,bsnd->bngqs', qb, k).astype(jnp.float32)
        p = jax.nn.softmax(sc, axis=-1).astype(v.dtype)
        return jnp.einsum('bngqs,bsnd->bqngd', p, v)

    y_meta = attend(q[:, :N_META]).reshape(b, N_META, GQA_HEADS * d)
    qb = jnp.transpose(q[:, N_META:].reshape(b, s // Q_BLOCK, Q_BLOCK, GQA_KV_HEADS, GQA_GROUP, d),
                       (1, 0, 2, 3, 4, 5))
    y_grid = lax.map(attend, qb)
    y_grid = jnp.transpose(y_grid, (1, 0, 2, 3, 4, 5)).reshape(b, s, GQA_HEADS * d)
    return jnp.concatenate([y_meta, y_grid], axis=1)


def _token_mixer(x, norm_pre, w_in, na_rel_bias, na_meta_bias, gqa_q_norm, gqa_k_norm,
                 grp_norm_a, grp_norm_b, w_out, norm_post):
    b, l, _ = x.shape
    h = _rmsnorm(x, norm_pre)
    proj = h @ w_in
    qa, ka, va, qb, kb, vb = jnp.split(proj, list(IN_SPLITS), axis=-1)
    shp_a = (b, l, NA_HEADS, HEAD_DIM)
    shp_kv = (b, l, GQA_KV_HEADS, HEAD_DIM)
    ya = _neighbourhood_attention(qa.reshape(shp_a), ka.reshape(shp_a), va.reshape(shp_a),
                                  na_rel_bias, na_meta_bias)
    yb = _axial_gqa(qb.reshape(b, l, GQA_HEADS, HEAD_DIM), kb.reshape(shp_kv), vb.reshape(shp_kv),
                    gqa_q_norm, gqa_k_norm)
    y = jnp.concatenate([_rmsnorm(ya, grp_norm_a), _rmsnorm(yb, grp_norm_b)], axis=-1) @ w_out
    return x + _rmsnorm(y, norm_post)


def setup_inputs(seed: int = 0) -> dict:
    key = jax.random.key(seed)
    ks = jax.random.split(key, 24)

    def nrm(k, shape, scale):
        return scale * jax.random.normal(k, shape, jnp.float32)

    def gain(k, shape):
        return 1.0 + 0.05 * jax.random.normal(k, shape, jnp.float32)

    return {
        'x_prompt': nrm(ks[0], (BATCH, SEQ, D_MODEL), 1.0),
        'x_sample': nrm(ks[1], (DEC_BATCH, DEC_SEQ, D_MODEL), 1.0),
        'meta_tokens': nrm(ks[2], (N_META, D_MODEL), 1.0),
        'ffn1_norm_pre': gain(ks[3], (DEPTH, D_MODEL)),
        'ffn1_w_gate': nrm(ks[4], (DEPTH, D_MODEL, D_FF), D_MODEL ** -0.5),
        'ffn1_w_up': nrm(ks[5], (DEPTH, D_MODEL, D_FF), D_MODEL ** -0.5),
        'ffn1_w_down': nrm(ks[6], (DEPTH, D_FF, D_MODEL), D_FF ** -0.5),
        'ffn1_norm_post': gain(ks[7], (DEPTH, D_MODEL)),
        'mix_norm_pre': gain(ks[8], (DEPTH, D_MODEL)),
        'w_in': nrm(ks[9], (DEPTH, D_MODEL, IN_COLS), D_MODEL ** -0.5),
        'na_rel_bias': nrm(ks[10], (DEPTH, NA_HEADS, 2 * NA_WIN_H - 1, 2 * NA_WIN_W - 1), 0.02),
        'na_meta_bias': nrm(ks[11], (DEPTH, NA_HEADS, N_META), 0.02),
        'gqa_q_norm': gain(ks[12], (DEPTH, HEAD_DIM)),
        'gqa_k_norm': gain(ks[13], (DEPTH, HEAD_DIM)),
        'grp_norm_a': gain(ks[14], (DEPTH, W_A)),
        'grp_norm_b': gain(ks[15], (DEPTH, W_B)),
        'w_out': nrm(ks[16], (DEPTH, MIX_WIDTH, D_MODEL), MIX_WIDTH ** -0.5),
        'mix_norm_post': gain(ks[17], (DEPTH, D_MODEL)),
        'ffn2_norm_pre': gain(ks[18], (DEPTH, D_MODEL)),
        'ffn2_w_gate': nrm(ks[19], (DEPTH, D_MODEL, D_FF), D_MODEL ** -0.5),
        'ffn2_w_up': nrm(ks[20], (DEPTH, D_MODEL, D_FF), D_MODEL ** -0.5),
        'ffn2_w_down': nrm(ks[21], (DEPTH, D_FF, D_MODEL), D_FF ** -0.5),
        'ffn2_norm_post': gain(ks[22], (DEPTH, D_MODEL)),
        'final_norm': gain(ks[23], (D_MODEL,)),
    }


def reference(x_prompt, x_sample, meta_tokens,
              ffn1_norm_pre, ffn1_w_gate, ffn1_w_up, ffn1_w_down, ffn1_norm_post,
              mix_norm_pre, w_in, na_rel_bias, na_meta_bias, gqa_q_norm, gqa_k_norm,
              grp_norm_a, grp_norm_b, w_out, mix_norm_post,
              ffn2_norm_pre, ffn2_w_gate, ffn2_w_up, ffn2_w_down, ffn2_norm_post,
              final_norm):
    def run(x):
        b = x.shape[0]
        meta = jnp.broadcast_to(meta_tokens.astype(x.dtype)[None], (b, N_META, D_MODEL))
        h = jnp.concatenate([meta, x], axis=1)
        for l in range(DEPTH):
            h = _half_ffn(h, ffn1_norm_pre[l], ffn1_w_gate[l], ffn1_w_up[l], ffn1_w_down[l], ffn1_norm_post[l])
            h = _token_mixer(h, mix_norm_pre[l], w_in[l], na_rel_bias[l], na_meta_bias[l],
                             gqa_q_norm[l], gqa_k_norm[l], grp_norm_a[l], grp_norm_b[l],
                             w_out[l], mix_norm_post[l])
            h = _half_ffn(h, ffn2_norm_pre[l], ffn2_w_gate[l], ffn2_w_up[l], ffn2_w_down[l], ffn2_norm_post[l])
        return _rmsnorm(h, final_norm)[:, N_META:]

    y_prompt = run(x_prompt)
    y_sample = run(x_sample)
    return (y_prompt, y_sample)
```

```python
import functools

import numpy as np
import jax
import jax.numpy as jnp
from jax import lax
from jax.experimental import pallas as pl
from jax.experimental.pallas import tpu as pltpu

N_META = 16
GRID_W = 64
HEAD_DIM = 64
NA_HEADS = 8
NA_WIN_H = 8
NA_WIN_W = 16
GQA_HEADS = 8
GQA_KV_HEADS = 2
GQA_GROUP = GQA_HEADS // GQA_KV_HEADS
W_A = NA_HEADS * HEAD_DIM
W_B = GQA_HEADS * HEAD_DIM
W_KV = GQA_KV_HEADS * HEAD_DIM
ROPE_THETA = 10000.0
ROPE_AXIS_DIM = HEAD_DIM // 2
EPS = 1e-6

LANES = 128
NEG = -1e30
META_PAD = 128
FF_CHUNK = 256
TOKEN_TILE = 512
NA_ROWS = 4
NA_BLOCK = NA_ROWS * GRID_W
GQA_Q_TILE = 256
GQA_KEY_CHUNK = 512
VMEM_LIMIT = 56 * 1024 * 1024

_NT = (((1,), (1,)), ((), ()))

f32 = jnp.float32
bf16 = jnp.bfloat16


def _rms(x, g):
    ms = jnp.mean(x * x, axis=-1, keepdims=True)
    return x * lax.rsqrt(ms + EPS) * g


def _const_spec(shape):
    nd = len(shape)
    return pl.BlockSpec(shape, lambda *_: (0,) * nd, pipeline_mode=pl.Buffered(1))


def _swiglu_half(x, npre, wg_ref, wu_ref, wd_ref, npost, act_ref):
    xn = _rms(x, npre).astype(bf16)
    d_ff = wg_ref.shape[1]
    for c0 in range(0, d_ff, FF_CHUNK):
        g = jnp.dot(xn, wg_ref[:, c0:c0 + FF_CHUNK], preferred_element_type=f32)
        u = jnp.dot(xn, wu_ref[:, c0:c0 + FF_CHUNK], preferred_element_type=f32)
        act_ref[:, c0:c0 + FF_CHUNK] = (g * jax.nn.sigmoid(g) * u).astype(bf16)
    y = jnp.dot(act_ref[...], wd_ref[...], preferred_element_type=f32)
    return x + 0.5 * _rms(y, npost)


def _norm_rope_slab(x, g, cos, sin, lane):
    sq = x * x
    lo = jnp.sum(jnp.where(lane < HEAD_DIM, sq, 0.0), axis=-1, keepdims=True)
    hi = jnp.sum(jnp.where(lane < HEAD_DIM, 0.0, sq), axis=-1, keepdims=True)
    ms = jnp.where(lane < HEAD_DIM, lo, hi) * (1.0 / HEAD_DIM)
    xn = x * lax.rsqrt(ms + EPS) * g
    partner = jnp.where((lane % 32) < 16, pltpu.roll(xn, LANES - 16, 1), pltpu.roll(xn, 16, 1))
    return xn * cos + partner * sin


def _ffn_in_kernel(x_ref, cos_ref, sin_ref, n1pre_ref, wg_ref, wu_ref, wd_ref, n1post_ref,
                   mixpre_ref, win_ref, qn_ref, kn_ref,
                   h1_ref, qa_ref, ka_ref, vat_ref, qb_ref, kb_ref, vbt_ref, act_ref,
                   *, na_chunk, gqa_chunk):
    tm = x_ref.shape[0]
    h1 = _swiglu_half(x_ref[...], n1pre_ref[...], wg_ref, wu_ref, wd_ref, n1post_ref[...], act_ref)
    h1_ref[...] = h1
    hn = _rms(h1, mixpre_ref[...]).astype(bf16)

    def proj(c0, width):
        return jnp.dot(hn, win_ref[:, c0:c0 + width], preferred_element_type=f32)

    scale = HEAD_DIM ** -0.5
    qa_ref[...] = (proj(0, W_A) * scale).astype(bf16)
    ka_ref[...] = proj(W_A, W_A).astype(bf16)
    va = proj(2 * W_A, W_A)
    for j in range(tm // na_chunk):
        vat_ref[j] = va[j * na_chunk:(j + 1) * na_chunk, :].T.astype(bf16)

    lane = lax.broadcasted_iota(jnp.int32, (tm, LANES), 1)
    cos = cos_ref[...]
    sin = sin_ref[...]
    qn = qn_ref[...]
    kn = kn_ref[...]
    for s in range(W_B // LANES):
        qs = proj(3 * W_A + s * LANES, LANES)
        qb_ref[:, s * LANES:(s + 1) * LANES] = (_norm_rope_slab(qs, qn, cos, sin, lane) * scale).astype(bf16)
    kb = _norm_rope_slab(proj(3 * W_A + W_B, W_KV), kn, cos, sin, lane)
    kb_ref[:, :LANES] = kb.astype(bf16)
    kb_ref[:, LANES:] = pltpu.roll(kb, HEAD_DIM, 1).astype(bf16)
    vb = proj(3 * W_A + W_B + W_KV, W_KV)
    for j in range(tm // gqa_chunk):
        vbt_ref[j] = vb[j * gqa_chunk:(j + 1) * gqa_chunk, :].T.astype(bf16)


def _ffn_in(x2, cos, sin, p, *, tm, na_chunk, gqa_chunk):
    t, d = x2.shape
    s_len = cos.shape[0]
    assert t % tm == 0 and s_len % tm == 0 and tm % na_chunk == 0 and tm % gqa_chunk == 0
    n_pos = s_len // tm
    d_ff = p['wg1'].shape[1]
    in_cols = p['w_in'].shape[1]
    tok = lambda w: pl.BlockSpec((tm, w), lambda i: (i, 0))
    pos = pl.BlockSpec((tm, LANES), lambda i: (i % n_pos, 0))
    out_shape = (
        jax.ShapeDtypeStruct((t, d), f32),
        jax.ShapeDtypeStruct((t, W_A), bf16),
        jax.ShapeDtypeStruct((t, W_A), bf16),
        jax.ShapeDtypeStruct((t // na_chunk, W_A, na_chunk), bf16),
        jax.ShapeDtypeStruct((t, W_B), bf16),
        jax.ShapeDtypeStruct((t, 2 * W_KV), bf16),
        jax.ShapeDtypeStruct((t // gqa_chunk, W_KV, gqa_chunk), bf16),
    )
    out_specs = (
        tok(d), tok(W_A), tok(W_A),
        pl.BlockSpec((tm // na_chunk, W_A, na_chunk), lambda i: (i, 0, 0)),
        tok(W_B), tok(2 * W_KV),
        pl.BlockSpec((tm // gqa_chunk, W_KV, gqa_chunk), lambda i: (i, 0, 0)),
    )
    in_specs = [
        tok(d), pos, pos,
        _const_spec((1, d)), _const_spec((d, d_ff)), _const_spec((d, d_ff)), _const_spec((d_ff, d)),
        _const_spec((1, d)), _const_spec((1, d)), _const_spec((d, in_cols)),
        _const_spec((1, LANES)), _const_spec((1, LANES)),
    ]
    return pl.pallas_call(
        functools.partial(_ffn_in_kernel, na_chunk=na_chunk, gqa_chunk=gqa_chunk),
        grid=(t // tm,),
        in_specs=in_specs,
        out_specs=out_specs,
        out_shape=out_shape,
        scratch_shapes=[pltpu.VMEM((tm, d_ff), bf16)],
        compiler_params=pltpu.CompilerParams(dimension_semantics=("arbitrary",),
                                             vmem_limit_bytes=VMEM_LIMIT),
        name="ffn_in",
    )(x2, cos, sin, p['n1pre'], p['wg1'], p['wu1'], p['wd1'], p['n1post'],
      p['mixpre'], p['w_in'], p['qn'], p['kn'])


def _na_kernel(q_ref, k0_ref, k1_ref, k2_ref, v0_ref, v1_ref, v2_ref, km_ref, vm_ref,
               strips_ref, mb_ref, g_ref, o_ref, ot_ref, *, n_rows):
    j = pl.program_id(1)
    n_blk = pl.num_programs(1)
    blk0 = jnp.clip(j - 1, 0, n_blk - 3)
    k_refs = (k0_ref, k1_ref, k2_ref)
    v_refs = (v0_ref, v1_ref, v2_ref)

    a = lax.broadcasted_iota(jnp.int32, (NA_BLOCK, NA_BLOCK), 0) // GRID_W
    b = lax.broadcasted_iota(jnp.int32, (NA_BLOCK, NA_BLOCK), 1) // GRID_W
    r0 = jnp.clip(NA_ROWS * j + b - NA_WIN_H // 2, 0, n_rows - NA_WIN_H)
    valid = []
    strip0 = []
    for p in range(3):
        kr = NA_ROWS * (blk0 + p) + a
        valid.append((kr >= r0) & (kr < r0 + NA_WIN_H))
        strip0.append(NA_ROWS * (blk0 + p - j) + NA_WIN_H // 2)
    lane = lax.broadcasted_iota(jnp.int32, (NA_BLOCK, LANES), 1)
    n_strips = strips_ref.shape[1]

    for h in range(NA_HEADS):
        sl = slice(LANES * (h // 2), LANES * (h // 2) + LANES)
        hs = slice(HEAD_DIM * h, HEAD_DIM * (h + 1))
        in_head = (lane < HEAD_DIM) if h % 2 == 0 else (lane >= HEAD_DIM)
        qs = q_ref[:, sl]
        qm = jnp.where(in_head, qs, jnp.zeros_like(qs))
        s_parts = []
        for p in range(3):
            s = lax.dot_general(k_refs[p][:, sl], qm, _NT, preferred_element_type=f32)
            bias = jnp.concatenate(
                [strips_ref[h, jnp.clip(strip0[p] + r, 0, n_strips - 1)] for r in range(NA_ROWS)], axis=0)
            s_parts.append(jnp.where(valid[p], s + bias, NEG))
        sm = lax.dot_general(km_ref[:, sl], qm, _NT, preferred_element_type=f32) + mb_ref[h]
        m = jnp.max(sm, axis=0, keepdims=True)
        for s in s_parts:
            m = jnp.maximum(m, jnp.max(s, axis=0, keepdims=True))
        pm = jnp.exp(sm - m)
        l = jnp.sum(pm, axis=0, keepdims=True)
        acc = jnp.dot(vm_ref[0, hs, :], pm.astype(bf16), preferred_element_type=f32)
        for p in range(3):
            pp = jnp.exp(s_parts[p] - m)
            l = l + jnp.sum(pp, axis=0, keepdims=True)
            acc = acc + jnp.dot(v_refs[p][0, hs, :], pp.astype(bf16), preferred_element_type=f32)
        ot_ref[hs, :] = acc / l
    o_ref[...] = _rms(ot_ref[...].T, g_ref[...]).astype(bf16)


def _na_attn(qa, ka, vat, km, vmt, strips, mb, g, *, batch, n_rows):
    t = qa.shape[0]
    assert n_rows % NA_ROWS == 0
    n_blk = n_rows // NA_ROWS
    assert n_blk >= 3 and t == batch * n_blk * NA_BLOCK

    def kv_idx(p):
        return lambda b, j: b * n_blk + jnp.clip(j - 1, 0, n_blk - 3) + p

    k_specs = [pl.BlockSpec((NA_BLOCK, W_A), (lambda f: lambda b, j: (f(b, j), 0))(kv_idx(p))) for p in range(3)]
    v_specs = [pl.BlockSpec((1, W_A, NA_BLOCK), (lambda f: lambda b, j: (f(b, j), 0, 0))(kv_idx(p)))
               for p in range(3)]
    q_spec = pl.BlockSpec((NA_BLOCK, W_A), lambda b, j: (b * n_blk + j, 0))
    return pl.pallas_call(
        functools.partial(_na_kernel, n_rows=n_rows),
        grid=(batch, n_blk),
        in_specs=[q_spec, *k_specs, *v_specs,
                  _const_spec(km.shape), _const_spec(vmt.shape), _const_spec(strips.shape),
                  _const_spec(mb.shape), _const_spec(g.shape)],
        out_specs=pl.BlockSpec((NA_BLOCK, W_A), lambda b, j: (b * n_blk + j, 0)),
        out_shape=jax.ShapeDtypeStruct((t, W_A), bf16),
        scratch_shapes=[pltpu.VMEM((W_A, NA_BLOCK), f32)],
        compiler_params=pltpu.CompilerParams(dimension_semantics=("arbitrary", "arbitrary"),
                                             vmem_limit_bytes=VMEM_LIMIT),
        name="na_attn",
    )(qa, ka, ka, ka, vat, vat, vat, km, vmt, strips, mb, g)


def _gqa_kernel(q_ref, k_ref, vt_ref, km_ref, vmt_ref, g_ref, o_ref, s_ref, ot_ref):
    tq = q_ref.shape[0]
    n_chunks = vt_ref.shape[0]
    kc = vt_ref.shape[2]
    lane = lax.broadcasted_iota(jnp.int32, (tq, LANES), 1)
    meta_row = lax.broadcasted_iota(jnp.int32, (META_PAD, tq), 0)

    for h in range(GQA_HEADS):
        n = h // GQA_GROUP
        half = h % 2
        sl = slice(LANES * (h // 2), LANES * (h // 2) + LANES)
        ksl = slice(0, LANES) if half == n else slice(LANES, 2 * LANES)
        vs = slice(HEAD_DIM * n, HEAD_DIM * (n + 1))
        in_head = (lane < HEAD_DIM) if half == 0 else (lane >= HEAD_DIM)
        qs = q_ref[:, sl]
        qm = jnp.where(in_head, qs, jnp.zeros_like(qs))

        def scores(c, mx):
            rows = pl.ds(pl.multiple_of(c * kc, kc), kc)
            s = lax.dot_general(k_ref[rows, ksl], qm, _NT, preferred_element_type=f32)
            s_ref[rows, :] = s
            return jnp.maximum(mx, jnp.max(s, axis=0, keepdims=True))

        sm = lax.dot_general(km_ref[:, ksl], qm, _NT, preferred_element_type=f32)
        sm = jnp.where(meta_row < N_META, sm, NEG)
        m = lax.fori_loop(0, n_chunks, scores, jnp.max(sm, axis=0, keepdims=True))

        def weighted(c, carry):
            l, acc = carry
            rows = pl.ds(pl.multiple_of(c * kc, kc), kc)
            p = jnp.exp(s_ref[rows, :] - m)
            l = l + jnp.sum(p, axis=0, keepdims=True)
            acc = acc + jnp.dot(vt_ref[c, vs, :], p.astype(bf16), preferred_element_type=f32)
            return l, acc

        pm = jnp.exp(sm - m)
        l0 = jnp.sum(pm, axis=0, keepdims=True)
        acc0 = jnp.dot(vmt_ref[0, vs, :], pm.astype(bf16), preferred_element_type=f32)
        l, acc = lax.fori_loop(0, n_chunks, weighted, (l0, acc0))
        ot_ref[HEAD_DIM * h:HEAD_DIM * (h + 1), :] = acc / l
    o_ref[...] = _rms(ot_ref[...].T, g_ref[...]).astype(bf16)


def _gqa_attn(qb, kb, vbt, km, vmt, g, *, batch, seq):
    t = qb.shape[0]
    tq = GQA_Q_TILE
    kc = vbt.shape[2]
    assert t == batch * seq and seq % tq == 0 and seq % kc == 0
    n_q = seq // tq
    return pl.pallas_call(
        _gqa_kernel,
        grid=(batch, n_q),
        in_specs=[pl.BlockSpec((tq, W_B), lambda b, i: (b * n_q + i, 0)),
                  pl.BlockSpec((seq, 2 * W_KV), lambda b, i: (b, 0)),
                  pl.BlockSpec((seq // kc, W_KV, kc), lambda b, i: (b, 0, 0)),
                  _const_spec(km.shape), _const_spec(vmt.shape), _const_spec(g.shape)],
        out_specs=pl.BlockSpec((tq, W_B), lambda b, i: (b * n_q + i, 0)),
        out_shape=jax.ShapeDtypeStruct((t, W_B), bf16),
        scratch_shapes=[pltpu.VMEM((seq, tq), f32), pltpu.VMEM((W_B, tq), f32)],
        compiler_params=pltpu.CompilerParams(dimension_semantics=("arbitrary", "arbitrary"),
                                             vmem_limit_bytes=VMEM_LIMIT),
        name="gqa_attn",
    )(qb, kb, vbt, km, vmt, g)


def _out_ffn_kernel(h1_ref, ya_ref, yb_ref, woa_ref, wob_ref, mixpost_ref,
                    n2pre_ref, wg_ref, wu_ref, wd_ref, n2post_ref, fin_ref, o_ref, act_ref):
    y = (jnp.dot(ya_ref[...], woa_ref[...], preferred_element_type=f32)
         + jnp.dot(yb_ref[...], wob_ref[...], preferred_element_type=f32))
    h2 = h1_ref[...] + _rms(y, mixpost_ref[...])
    h3 = _swiglu_half(h2, n2pre_ref[...], wg_ref, wu_ref, wd_ref, n2post_ref[...], act_ref)
    o_ref[...] = _rms(h3, fin_ref[...])


def _out_ffn(h1, ya, yb, p, *, tm):
    t, d = h1.shape
    d_ff = p['wg2'].shape[1]
    tok = lambda w: pl.BlockSpec((tm, w), lambda i: (i, 0))
    return pl.pallas_call(
        _out_ffn_kernel,
        grid=(t // tm,),
        in_specs=[tok(d), tok(W_A), tok(W_B),
                  _const_spec((W_A, d)), _const_spec((W_B, d)), _const_spec((1, d)),
                  _const_spec((1, d)), _const_spec((d, d_ff)), _const_spec((d, d_ff)),
                  _const_spec((d_ff, d)), _const_spec((1, d)), _const_spec((1, d))],
        out_specs=tok(d),
        out_shape=jax.ShapeDtypeStruct((t, d), f32),
        scratch_shapes=[pltpu.VMEM((tm, d_ff), bf16)],
        compiler_params=pltpu.CompilerParams(dimension_semantics=("arbitrary",),
                                             vmem_limit_bytes=VMEM_LIMIT),
        name="out_ffn",
    )(h1, ya, yb, p['woa'], p['wob'], p['mixpost'], p['n2pre'], p['wg2'], p['wu2'], p['wd2'],
      p['n2post'], p['fin'])


def _rope_tables(pos_row, pos_col):
    inv_freq = jnp.asarray(ROPE_THETA ** (-np.arange(0, ROPE_AXIS_DIM, 2) / ROPE_AXIS_DIM), f32)
    ang_r = pos_row[:, None] * inv_freq[None, :]
    ang_c = pos_col[:, None] * inv_freq[None, :]
    cos = jnp.concatenate([jnp.cos(ang_r)] * 2 + [jnp.cos(ang_c)] * 2, axis=-1)
    sin = jnp.concatenate([-jnp.sin(ang_r), jnp.sin(ang_r), -jnp.sin(ang_c), jnp.sin(ang_c)], axis=-1)
    return jnp.tile(cos, (1, 2)), jnp.tile(sin, (1, 2))


def _bias_strips(rel_bias):
    n_d = 2 * NA_WIN_H - 1
    d = np.arange(3, n_d)[:, None, None, None]
    c = np.arange(GRID_W)[None, :, None, None]
    b = np.arange(NA_ROWS)[None, None, :, None]
    q = np.arange(GRID_W)[None, None, None, :]
    col_start = np.clip(q - NA_WIN_W // 2, 0, GRID_W - NA_WIN_W)
    ok = (d - b >= 0) & (d - b < n_d) & (c >= col_start) & (c < col_start + NA_WIN_W)
    shape = np.broadcast_shapes(d.shape, c.shape, b.shape, q.shape)
    ri = np.broadcast_to(np.clip(d - b, 0, n_d - 1), shape)
    ci = np.broadcast_to(np.clip(c - q + NA_WIN_W - 1, 0, 2 * NA_WIN_W - 2), shape)
    vals = rel_bias.astype(f32)[:, ri, ci]
    vals = jnp.where(np.broadcast_to(ok, shape)[None], vals, NEG)
    return vals.reshape(rel_bias.shape[0], shape[0], GRID_W, NA_ROWS * GRID_W)


def kernel(x_prompt, x_sample, meta_tokens, ffn1_norm_pre, ffn1_w_gate, ffn1_w_up, ffn1_w_down, ffn1_norm_post, mix_norm_pre, w_in, na_rel_bias, na_meta_bias, gqa_q_norm, gqa_k_norm, grp_norm_a, grp_norm_b, w_out, mix_norm_post, ffn2_norm_pre, ffn2_w_gate, ffn2_w_up, ffn2_w_down, ffn2_norm_post, final_norm):
    assert ffn1_w_gate.shape[0] == 1, "meta keys/values are shared across the batch only for depth 1"
    d = x_prompt.shape[-1]
    row = lambda v: v.reshape(1, -1).astype(f32)
    p = dict(
        n1pre=row(ffn1_norm_pre[0]), wg1=ffn1_w_gate[0].astype(bf16), wu1=ffn1_w_up[0].astype(bf16),
        wd1=ffn1_w_down[0].astype(bf16), n1post=row(ffn1_norm_post[0]),
        mixpre=row(mix_norm_pre[0]), w_in=w_in[0].astype(bf16),
        qn=row(jnp.tile(gqa_q_norm[0], 2)), kn=row(jnp.tile(gqa_k_norm[0], 2)),
        woa=w_out[0, :W_A].astype(bf16), wob=w_out[0, W_A:].astype(bf16), mixpost=row(mix_norm_post[0]),
        n2pre=row(ffn2_norm_pre[0]), wg2=ffn2_w_gate[0].astype(bf16), wu2=ffn2_w_up[0].astype(bf16),
        wd2=ffn2_w_down[0].astype(bf16), n2post=row(ffn2_norm_post[0]), fin=row(final_norm),
    )
    ga = row(grp_norm_a[0])
    gb = row(grp_norm_b[0])
    strips = _bias_strips(na_rel_bias[0])
    mb = jnp.where(np.arange(META_PAD)[None, :, None] < N_META,
                   jnp.pad(na_meta_bias[0].astype(f32), ((0, 0), (0, META_PAD - N_META)))[:, :, None], NEG)
    mb = jnp.broadcast_to(mb, (NA_HEADS, META_PAD, NA_BLOCK))

    xm = jnp.pad(meta_tokens.astype(f32), ((0, META_PAD - N_META), (0, 0)))
    mcos, msin = _rope_tables(jnp.pad(-jnp.ones((N_META,), f32), (0, META_PAD - N_META)),
                              jnp.pad(jnp.arange(N_META, dtype=f32), (0, META_PAD - N_META)))
    _, _, kam, vamt, _, kbm, vbmt = _ffn_in(xm, mcos, msin, p, tm=META_PAD, na_chunk=META_PAD,
                                            gqa_chunk=META_PAD)

    def run(x):
        batch, seq, _ = x.shape
        n_rows = seq // GRID_W
        tpos = jnp.arange(seq)
        cos, sin = _rope_tables((tpos // GRID_W).astype(f32), (tpos % GRID_W).astype(f32))
        h1, qa, ka, vat, qb, kb, vbt = _ffn_in(x.reshape(batch * seq, d), cos, sin, p, tm=TOKEN_TILE,
                                               na_chunk=NA_BLOCK, gqa_chunk=GQA_KEY_CHUNK)
        ya = _na_attn(qa, ka, vat, kam, vamt, strips, mb, ga, batch=batch, n_rows=n_rows)
        yb = _gqa_attn(qb, kb, vbt, kbm, vbmt, gb, batch=batch, seq=seq)
        return _out_ffn(h1, ya, yb, p, tm=TOKEN_TILE).reshape(batch, seq, d)

    return (run(x_prompt), run(x_sample))
```

```python
import functools

import numpy as np
import jax
import jax.numpy as jnp
from jax import lax
from jax.experimental import pallas as pl
from jax.experimental.pallas import tpu as pltpu

N_META = 16
GRID_W = 64
HEAD_DIM = 64
NA_HEADS = 8
NA_WIN_H = 8
NA_WIN_W = 16
GQA_HEADS = 8
GQA_KV_HEADS = 2
GQA_GROUP = GQA_HEADS // GQA_KV_HEADS
W_A = NA_HEADS * HEAD_DIM
W_B = GQA_HEADS * HEAD_DIM
W_KV = GQA_KV_HEADS * HEAD_DIM
ROPE_THETA = 10000.0
ROPE_AXIS_DIM = HEAD_DIM // 2
EPS = 1e-6

LANES = 128
NEG = -1e30
META_PAD = 128
FF_CHUNK = 256
TOKEN_TILE = 512
NA_ROWS = 4
NA_BLOCK = NA_ROWS * GRID_W
STRIP_D0 = NA_ROWS - 1
GQA_Q_TILE = 256
GQA_KEY_CHUNK = 4096
VMEM_LIMIT = 56 * 1024 * 1024

_NT = (((1,), (1,)), ((), ()))

f32 = jnp.float32
bf16 = jnp.bfloat16


def _rms(x, g):
    ms = jnp.mean(x * x, axis=-1, keepdims=True)
    return x * lax.rsqrt(ms + EPS) * g


def _const_spec(shape):
    nd = len(shape)
    return pl.BlockSpec(shape, lambda *_: (0,) * nd, pipeline_mode=pl.Buffered(1))


def _swiglu_half(x, npre, wg_ref, wu_ref, wd_ref, npost, act_ref):
    xn = _rms(x, npre).astype(bf16)
    d_ff = wg_ref.shape[1]
    for c0 in range(0, d_ff, FF_CHUNK):
        g = jnp.dot(xn, wg_ref[:, c0:c0 + FF_CHUNK], preferred_element_type=f32)
        u = jnp.dot(xn, wu_ref[:, c0:c0 + FF_CHUNK], preferred_element_type=f32)
        act_ref[:, c0:c0 + FF_CHUNK] = (g * jax.nn.sigmoid(g) * u).astype(bf16)
    y = jnp.dot(act_ref[...], wd_ref[...], preferred_element_type=f32)
    return x + 0.5 * _rms(y, npost)


def _norm_rope_slab(x, g, cos, sin, lane):
    sq = x * x
    lo = jnp.sum(jnp.where(lane < HEAD_DIM, sq, 0.0), axis=-1, keepdims=True)
    hi = jnp.sum(jnp.where(lane < HEAD_DIM, 0.0, sq), axis=-1, keepdims=True)
    ms = jnp.where(lane < HEAD_DIM, lo, hi) * (1.0 / HEAD_DIM)
    xn = x * lax.rsqrt(ms + EPS) * g
    partner = jnp.where((lane % 32) < 16, pltpu.roll(xn, LANES - 16, 1), pltpu.roll(xn, 16, 1))
    return xn * cos + partner * sin


def _ffn_in_kernel(x_ref, cos_ref, sin_ref, n1pre_ref, wg_ref, wu_ref, wd_ref, n1post_ref,
                   mixpre_ref, win_ref, qn_ref, kn_ref,
                   h1_ref, qa_ref, ka_ref, vat_ref, qb_ref, kb_ref, vbt_ref, act_ref,
                   *, na_chunk):
    tm = x_ref.shape[0]
    h1 = _swiglu_half(x_ref[...], n1pre_ref[...], wg_ref, wu_ref, wd_ref, n1post_ref[...], act_ref)
    h1_ref[...] = h1
    hn = _rms(h1, mixpre_ref[...]).astype(bf16)

    def proj(c0, width):
        return jnp.dot(hn, win_ref[:, c0:c0 + width], preferred_element_type=f32)

    scale = HEAD_DIM ** -0.5
    qa_ref[...] = (proj(0, W_A) * scale).astype(bf16)
    ka_ref[...] = proj(W_A, W_A).astype(bf16)
    va = proj(2 * W_A, W_A)
    for j in range(tm // na_chunk):
        vat_ref[j] = va[j * na_chunk:(j + 1) * na_chunk, :].T.astype(bf16)

    lane = lax.broadcasted_iota(jnp.int32, (tm, LANES), 1)
    cos = cos_ref[...]
    sin = sin_ref[...]
    qn = qn_ref[...]
    kn = kn_ref[...]
    for s in range(W_B // LANES):
        qs = proj(3 * W_A + s * LANES, LANES)
        qb_ref[:, s * LANES:(s + 1) * LANES] = (_norm_rope_slab(qs, qn, cos, sin, lane) * scale).astype(bf16)
    kb = _norm_rope_slab(proj(3 * W_A + W_B, W_KV), kn, cos, sin, lane)
    kb_ref[:, :LANES] = kb.astype(bf16)
    kb_ref[:, LANES:] = pltpu.roll(kb, HEAD_DIM, 1).astype(bf16)
    vbt_ref[0] = proj(3 * W_A + W_B + W_KV, W_KV).T.astype(bf16)


def _ffn_in(x2, cos, sin, p, *, tm, na_chunk, gqa_chunk):
    t, d = x2.shape
    s_len = cos.shape[0]
    assert t % tm == 0 and s_len % tm == 0 and tm % na_chunk == 0
    assert gqa_chunk % tm == 0 and t % gqa_chunk == 0
    n_pos = s_len // tm
    tiles_per_chunk = gqa_chunk // tm
    d_ff = p['wg1'].shape[1]
    in_cols = p['w_in'].shape[1]
    tok = lambda w: pl.BlockSpec((tm, w), lambda i: (i, 0))
    pos = pl.BlockSpec((tm, LANES), lambda i: (i % n_pos, 0))
    out_shape = (
        jax.ShapeDtypeStruct((t, d), f32),
        jax.ShapeDtypeStruct((t, W_A), bf16),
        jax.ShapeDtypeStruct((t, W_A), bf16),
        jax.ShapeDtypeStruct((t // na_chunk, W_A, na_chunk), bf16),
        jax.ShapeDtypeStruct((t, W_B), bf16),
        jax.ShapeDtypeStruct((t, 2 * W_KV), bf16),
        jax.ShapeDtypeStruct((t // gqa_chunk, W_KV, gqa_chunk), bf16),
    )
    out_specs = (
        tok(d), tok(W_A), tok(W_A),
        pl.BlockSpec((tm // na_chunk, W_A, na_chunk), lambda i: (i, 0, 0)),
        tok(W_B), tok(2 * W_KV),
        pl.BlockSpec((1, W_KV, tm), lambda i: (i // tiles_per_chunk, 0, i % tiles_per_chunk)),
    )
    in_specs = [
        tok(d), pos, pos,
        _const_spec((1, d)), _const_spec((d, d_ff)), _const_spec((d, d_ff)), _const_spec((d_ff, d)),
        _const_spec((1, d)), _const_spec((1, d)), _const_spec((d, in_cols)),
        _const_spec((1, LANES)), _const_spec((1, LANES)),
    ]
    return pl.pallas_call(
        functools.partial(_ffn_in_kernel, na_chunk=na_chunk),
        grid=(t // tm,),
        in_specs=in_specs,
        out_specs=out_specs,
        out_shape=out_shape,
        scratch_shapes=[pltpu.VMEM((tm, d_ff), bf16)],
        compiler_params=pltpu.CompilerParams(dimension_semantics=("arbitrary",),
                                             vmem_limit_bytes=VMEM_LIMIT),
        name="ffn_in",
    )(x2, cos, sin, p['n1pre'], p['wg1'], p['wu1'], p['wd1'], p['n1post'],
      p['mixpre'], p['w_in'], p['qn'], p['kn'])


def _na_kernel(q_ref, k0_ref, k1_ref, k2_ref, v0_ref, v1_ref, v2_ref, km_ref, vm_ref,
               strips_ref, mb_ref, g_ref, o_ref, ot_ref, *, n_rows):
    j = pl.program_id(1)
    n_blk = pl.num_programs(1)
    blk0 = jnp.clip(j - 1, 0, n_blk - 3)
    k_refs = (k0_ref, k1_ref, k2_ref)
    v_refs = (v0_ref, v1_ref, v2_ref)

    a = lax.broadcasted_iota(jnp.int32, (NA_BLOCK, NA_BLOCK), 0) // GRID_W
    b = lax.broadcasted_iota(jnp.int32, (NA_BLOCK, NA_BLOCK), 1) // GRID_W
    r0 = jnp.clip(NA_ROWS * j + b - NA_WIN_H // 2, 0, n_rows - NA_WIN_H)
    valid = []
    strip0 = []
    for p in range(3):
        kr = NA_ROWS * (blk0 + p) + a
        valid.append((kr >= r0) & (kr < r0 + NA_WIN_H))
        strip0.append(NA_ROWS * (blk0 + p - j) + NA_WIN_H - 1 - STRIP_D0)
    lane = lax.broadcasted_iota(jnp.int32, (NA_BLOCK, LANES), 1)
    n_strips = strips_ref.shape[1]

    for h in range(NA_HEADS):
        sl = slice(LANES * (h // 2), LANES * (h // 2) + LANES)
        hs = slice(HEAD_DIM * h, HEAD_DIM * (h + 1))
        in_head = (lane < HEAD_DIM) if h % 2 == 0 else (lane >= HEAD_DIM)
        qs = q_ref[:, sl]
        qm = jnp.where(in_head, qs, jnp.zeros_like(qs))
        s_parts = []
        for p in range(3):
            s = lax.dot_general(k_refs[p][:, sl], qm, _NT, preferred_element_type=f32)
            bias = jnp.concatenate(
                [strips_ref[h, jnp.clip(strip0[p] + r, 0, n_strips - 1)] for r in range(NA_ROWS)], axis=0)
            s_parts.append(jnp.where(valid[p], s + bias, NEG))
        sm = lax.dot_general(km_ref[:, sl], qm, _NT, preferred_element_type=f32) + mb_ref[h]
        m = jnp.max(sm, axis=0, keepdims=True)
        for s in s_parts:
            m = jnp.maximum(m, jnp.max(s, axis=0, keepdims=True))
        pm = jnp.exp(sm - m)
        l = jnp.sum(pm, axis=0, keepdims=True)
        acc = jnp.dot(vm_ref[0, hs, :], pm.astype(bf16), preferred_element_type=f32)
        for p in range(3):
            pp = jnp.exp(s_parts[p] - m)
            l = l + jnp.sum(pp, axis=0, keepdims=True)
            acc = acc + jnp.dot(v_refs[p][0, hs, :], pp.astype(bf16), preferred_element_type=f32)
        ot_ref[hs, :] = acc / l
    o_ref[...] = _rms(ot_ref[...].T, g_ref[...]).astype(bf16)


def _na_attn(qa, ka, vat, km, vmt, strips, mb, g, *, batch, n_rows):
    t = qa.shape[0]
    assert n_rows % NA_ROWS == 0
    n_blk = n_rows // NA_ROWS
    assert n_blk >= 3 and t == batch * n_blk * NA_BLOCK

    def kv_idx(p):
        return lambda b, j: b * n_blk + jnp.clip(j - 1, 0, n_blk - 3) + p

    k_specs = [pl.BlockSpec((NA_BLOCK, W_A), (lambda f: lambda b, j: (f(b, j), 0))(kv_idx(p))) for p in range(3)]
    v_specs = [pl.BlockSpec((1, W_A, NA_BLOCK), (lambda f: lambda b, j: (f(b, j), 0, 0))(kv_idx(p)))
               for p in range(3)]
    q_spec = pl.BlockSpec((NA_BLOCK, W_A), lambda b, j: (b * n_blk + j, 0))
    return pl.pallas_call(
        functools.partial(_na_kernel, n_rows=n_rows),
        grid=(batch, n_blk),
        in_specs=[q_spec, *k_specs, *v_specs,
                  _const_spec(km.shape), _const_spec(vmt.shape), _const_spec(strips.shape),
                  _const_spec(mb.shape), _const_spec(g.shape)],
        out_specs=pl.BlockSpec((NA_BLOCK, W_A), lambda b, j: (b * n_blk + j, 0)),
        out_shape=jax.ShapeDtypeStruct((t, W_A), bf16),
        scratch_shapes=[pltpu.VMEM((W_A, NA_BLOCK), f32)],
        compiler_params=pltpu.CompilerParams(dimension_semantics=("arbitrary", "arbitrary"),
                                             vmem_limit_bytes=VMEM_LIMIT),
        name="na_attn",
    )(qa, ka, ka, ka, vat, vat, vat, km, vmt, strips, mb, g)


def _loop(trips, body, init):
    return body(0, init) if trips == 1 else lax.fori_loop(0, trips, body, init)


def _chunk_rows(c, size):
    start = c * size
    return pl.ds(start if isinstance(c, int) else pl.multiple_of(start, size), size)


def _gqa_kernel(q_ref, k_ref, vt_ref, km_ref, vmt_ref, g_ref, o_ref, s_ref, ot_ref):
    tq = q_ref.shape[0]
    n_chunks = vt_ref.shape[0]
    kc = vt_ref.shape[2]
    lane = lax.broadcasted_iota(jnp.int32, (tq, LANES), 1)
    meta_row = lax.broadcasted_iota(jnp.int32, (META_PAD, tq), 0)

    for h in range(GQA_HEADS):
        n = h // GQA_GROUP
        half = h % 2
        sl = slice(LANES * (h // 2), LANES * (h // 2) + LANES)
        ksl = slice(0, LANES) if half == n else slice(LANES, 2 * LANES)
        vs = slice(HEAD_DIM * n, HEAD_DIM * (n + 1))
        in_head = (lane < HEAD_DIM) if half == 0 else (lane >= HEAD_DIM)
        qs = q_ref[:, sl]
        qm = jnp.where(in_head, qs, jnp.zeros_like(qs))

        def scores(c, mx):
            rows = _chunk_rows(c, kc)
            s = lax.dot_general(k_ref[rows, ksl], qm, _NT, preferred_element_type=f32)
            s_ref[rows, :] = s
            return jnp.maximum(mx, jnp.max(s, axis=0, keepdims=True))

        sm = lax.dot_general(km_ref[:, ksl], qm, _NT, preferred_element_type=f32)
        sm = jnp.where(meta_row < N_META, sm, NEG)
        m = _loop(n_chunks, scores, jnp.max(sm, axis=0, keepdims=True))

        def weighted(c, carry):
            l, acc = carry
            p = jnp.exp(s_ref[_chunk_rows(c, kc), :] - m)
            l = l + jnp.sum(p, axis=0, keepdims=True)
            acc = acc + jnp.dot(vt_ref[c, vs, :], p.astype(bf16), preferred_element_type=f32)
            return l, acc

        pm = jnp.exp(sm - m)
        l0 = jnp.sum(pm, axis=0, keepdims=True)
        acc0 = jnp.dot(vmt_ref[0, vs, :], pm.astype(bf16), preferred_element_type=f32)
        l, acc = _loop(n_chunks, weighted, (l0, acc0))
        ot_ref[HEAD_DIM * h:HEAD_DIM * (h + 1), :] = acc / l
    o_ref[...] = _rms(ot_ref[...].T, g_ref[...]).astype(bf16)


def _gqa_attn(qb, kb, vbt, km, vmt, g, *, batch, seq):
    t = qb.shape[0]
    tq = GQA_Q_TILE
    kc = vbt.shape[2]
    assert t == batch * seq and seq % tq == 0 and seq % kc == 0
    n_q = seq // tq
    return pl.pallas_call(
        _gqa_kernel,
        grid=(batch, n_q),
        in_specs=[pl.BlockSpec((tq, W_B), lambda b, i: (b * n_q + i, 0)),
                  pl.BlockSpec((seq, 2 * W_KV), lambda b, i: (b, 0)),
                  pl.BlockSpec((seq // kc, W_KV, kc), lambda b, i: (b, 0, 0)),
                  _const_spec(km.shape), _const_spec(vmt.shape), _const_spec(g.shape)],
        out_specs=pl.BlockSpec((tq, W_B), lambda b, i: (b * n_q + i, 0)),
        out_shape=jax.ShapeDtypeStruct((t, W_B), bf16),
        scratch_shapes=[pltpu.VMEM((seq, tq), f32), pltpu.VMEM((W_B, tq), f32)],
        compiler_params=pltpu.CompilerParams(dimension_semantics=("arbitrary", "arbitrary"),
                                             vmem_limit_bytes=VMEM_LIMIT),
        name="gqa_attn",
    )(qb, kb, vbt, km, vmt, g)


def _out_ffn_kernel(h1_ref, ya_ref, yb_ref, woa_ref, wob_ref, mixpost_ref,
                    n2pre_ref, wg_ref, wu_ref, wd_ref, n2post_ref, fin_ref, o_ref, act_ref):
    y = (jnp.dot(ya_ref[...], woa_ref[...], preferred_element_type=f32)
         + jnp.dot(yb_ref[...], wob_ref[...], preferred_element_type=f32))
    h2 = h1_ref[...] + _rms(y, mixpost_ref[...])
    h3 = _swiglu_half(h2, n2pre_ref[...], wg_ref, wu_ref, wd_ref, n2post_ref[...], act_ref)
    o_ref[...] = _rms(h3, fin_ref[...])


def _out_ffn(h1, ya, yb, p, *, tm):
    t, d = h1.shape
    d_ff = p['wg2'].shape[1]
    tok = lambda w: pl.BlockSpec((tm, w), lambda i: (i, 0))
    return pl.pallas_call(
        _out_ffn_kernel,
        grid=(t // tm,),
        in_specs=[tok(d), tok(W_A), tok(W_B),
                  _const_spec((W_A, d)), _const_spec((W_B, d)), _const_spec((1, d)),
                  _const_spec((1, d)), _const_spec((d, d_ff)), _const_spec((d, d_ff)),
                  _const_spec((d_ff, d)), _const_spec((1, d)), _const_spec((1, d))],
        out_specs=tok(d),
        out_shape=jax.ShapeDtypeStruct((t, d), f32),
        scratch_shapes=[pltpu.VMEM((tm, d_ff), bf16)],
        compiler_params=pltpu.CompilerParams(dimension_semantics=("arbitrary",),
                                             vmem_limit_bytes=VMEM_LIMIT),
        name="out_ffn",
    )(h1, ya, yb, p['woa'], p['wob'], p['mixpost'], p['n2pre'], p['wg2'], p['wu2'], p['wd2'],
      p['n2post'], p['fin'])


def _rope_tables(pos_row, pos_col):
    inv_freq = jnp.asarray(ROPE_THETA ** (-np.arange(0, ROPE_AXIS_DIM, 2) / ROPE_AXIS_DIM), f32)
    ang_r = pos_row[:, None] * inv_freq[None, :]
    ang_c = pos_col[:, None] * inv_freq[None, :]
    cos = jnp.concatenate([jnp.cos(ang_r)] * 2 + [jnp.cos(ang_c)] * 2, axis=-1)
    sin = jnp.concatenate([-jnp.sin(ang_r), jnp.sin(ang_r), -jnp.sin(ang_c), jnp.sin(ang_c)], axis=-1)
    return jnp.tile(cos, (1, 2)), jnp.tile(sin, (1, 2))


def _bias_strips(rel_bias):
    n_h, n_d, n_k = rel_bias.shape
    assert n_d == 2 * NA_WIN_H - 1 and n_k == 2 * NA_WIN_W - 1
    period = 2 * GRID_W - 1
    lead = GRID_W - NA_WIN_W
    z = jnp.concatenate([jnp.zeros((n_h, n_d, lead), f32), rel_bias.astype(f32)[..., ::-1],
                         jnp.zeros((n_h, n_d, period - lead - n_k), f32)], axis=-1)
    hank = jnp.tile(z, (1, 1, GRID_W + 1))[..., :GRID_W * (period + 1)]
    toep = hank.reshape(n_h, n_d, GRID_W, period + 1)[..., ::-1, :GRID_W]
    c = np.arange(GRID_W)[:, None]
    q = np.arange(GRID_W)[None, :]
    col_start = np.clip(q - NA_WIN_W // 2, 0, GRID_W - NA_WIN_W)
    toep = jnp.where((c >= col_start) & (c < col_start + NA_WIN_W), toep, NEG)
    n_strips = n_d - STRIP_D0
    return jnp.concatenate([toep[:, STRIP_D0 - b:STRIP_D0 - b + n_strips] for b in range(NA_ROWS)], axis=-1)


def kernel(x_prompt, x_sample, meta_tokens, ffn1_norm_pre, ffn1_w_gate, ffn1_w_up, ffn1_w_down, ffn1_norm_post, mix_norm_pre, w_in, na_rel_bias, na_meta_bias, gqa_q_norm, gqa_k_norm, grp_norm_a, grp_norm_b, w_out, mix_norm_post, ffn2_norm_pre, ffn2_w_gate, ffn2_w_up, ffn2_w_down, ffn2_norm_post, final_norm):
    assert ffn1_w_gate.shape[0] == 1, "meta keys/values are shared across the batch only for depth 1"
    d = x_prompt.shape[-1]
    row = lambda v: v.reshape(1, -1).astype(f32)
    p = dict(
        n1pre=row(ffn1_norm_pre[0]), wg1=ffn1_w_gate[0].astype(bf16), wu1=ffn1_w_up[0].astype(bf16),
        wd1=ffn1_w_down[0].astype(bf16), n1post=row(ffn1_norm_post[0]),
        mixpre=row(mix_norm_pre[0]), w_in=w_in[0].astype(bf16),
        qn=row(jnp.tile(gqa_q_norm[0], 2)), kn=row(jnp.tile(gqa_k_norm[0], 2)),
        woa=w_out[0, :W_A].astype(bf16), wob=w_out[0, W_A:].astype(bf16), mixpost=row(mix_norm_post[0]),
        n2pre=row(ffn2_norm_pre[0]), wg2=ffn2_w_gate[0].astype(bf16), wu2=ffn2_w_up[0].astype(bf16),
        wd2=ffn2_w_down[0].astype(bf16), n2post=row(ffn2_norm_post[0]), fin=row(final_norm),
    )
    ga = row(grp_norm_a[0])
    gb = row(grp_norm_b[0])
    strips = _bias_strips(na_rel_bias[0])
    mb = jnp.where(np.arange(META_PAD)[None, :, None] < N_META,
                   jnp.pad(na_meta_bias[0].astype(f32), ((0, 0), (0, META_PAD - N_META)))[:, :, None], NEG)
    mb = jnp.broadcast_to(mb, (NA_HEADS, META_PAD, NA_BLOCK))

    xm = jnp.pad(meta_tokens.astype(f32), ((0, META_PAD - N_META), (0, 0)))
    mcos, msin = _rope_tables(jnp.pad(-jnp.ones((N_META,), f32), (0, META_PAD - N_META)),
                              jnp.pad(jnp.arange(N_META, dtype=f32), (0, META_PAD - N_META)))
    _, _, kam, vamt, _, kbm, vbmt = _ffn_in(xm, mcos, msin, p, tm=META_PAD, na_chunk=META_PAD,
                                            gqa_chunk=META_PAD)

    def run(x):
        batch, seq, _ = x.shape
        n_rows = seq // GRID_W
        tpos = jnp.arange(seq)
        cos, sin = _rope_tables((tpos // GRID_W).astype(f32), (tpos % GRID_W).astype(f32))
        h1, qa, ka, vat, qb, kb, vbt = _ffn_in(x.reshape(batch * seq, d), cos, sin, p, tm=TOKEN_TILE,
                                               na_chunk=NA_BLOCK,
                                               gqa_chunk=GQA_KEY_CHUNK if seq % GQA_KEY_CHUNK == 0 else seq)
        ya = _na_attn(qa, ka, vat, kam, vamt, strips, mb, ga, batch=batch, n_rows=n_rows)
        yb = _gqa_attn(qb, kb, vbt, kbm, vbmt, gb, batch=batch, seq=seq)
        return _out_ffn(h1, ya, yb, p, tm=TOKEN_TILE).reshape(batch, seq, d)

    return (run(x_prompt), run(x_sample))
```

```python
import functools

import numpy as np
import jax
import jax.numpy as jnp
from jax import lax
from jax.experimental import pallas as pl
from jax.experimental.pallas import tpu as pltpu

N_META = 16
GRID_W = 64
HEAD_DIM = 64
NA_HEADS = 8
NA_WIN_H = 8
NA_WIN_W = 16
GQA_HEADS = 8
GQA_KV_HEADS = 2
GQA_GROUP = GQA_HEADS // GQA_KV_HEADS
W_A = NA_HEADS * HEAD_DIM
W_B = GQA_HEADS * HEAD_DIM
W_KV = GQA_KV_HEADS * HEAD_DIM
ROPE_THETA = 10000.0
ROPE_AXIS_DIM = HEAD_DIM // 2
EPS = 1e-6

LANES = 128
NEG = -1e30
LOG2_E = 1.4426950408889634
META_PAD = 128
FF_CHUNK = 256
TOKEN_TILE = 512
NA_ROWS = 4
NA_BLOCK = NA_ROWS * GRID_W
NA_KEY_ROWS = 3 * NA_ROWS
assert NA_ROWS == NA_WIN_H // 2
STRIP_D0 = NA_ROWS - 1
GQA_Q_TILE = 256
GQA_KEY_CHUNK = 4096
VMEM_LIMIT = 56 * 1024 * 1024

_NT = (((1,), (1,)), ((), ()))

f32 = jnp.float32
bf16 = jnp.bfloat16


def _rms(x, g):
    ms = jnp.mean(x * x, axis=-1, keepdims=True)
    return x * lax.rsqrt(ms + EPS) * g


def _const_spec(shape):
    nd = len(shape)
    return pl.BlockSpec(shape, lambda *_: (0,) * nd, pipeline_mode=pl.Buffered(1))


def _swiglu_half(x, npre, wg_ref, wu_ref, wd_ref, npost, act_ref):
    xn = _rms(x, npre).astype(bf16)
    d_ff = wg_ref.shape[1]
    for c0 in range(0, d_ff, FF_CHUNK):
        g = jnp.dot(xn, wg_ref[:, c0:c0 + FF_CHUNK], preferred_element_type=f32)
        u = jnp.dot(xn, wu_ref[:, c0:c0 + FF_CHUNK], preferred_element_type=f32)
        act_ref[:, c0:c0 + FF_CHUNK] = (g * jax.nn.sigmoid(g) * u).astype(bf16)
    y = jnp.dot(act_ref[...], wd_ref[...], preferred_element_type=f32)
    return x + 0.5 * _rms(y, npost)


def _norm_rope_slab(x, g, cos, sin, lane):
    sq = x * x
    lo = jnp.sum(jnp.where(lane < HEAD_DIM, sq, 0.0), axis=-1, keepdims=True)
    hi = jnp.sum(jnp.where(lane < HEAD_DIM, 0.0, sq), axis=-1, keepdims=True)
    ms = jnp.where(lane < HEAD_DIM, lo, hi) * (1.0 / HEAD_DIM)
    xn = x * lax.rsqrt(ms + EPS) * g
    partner = jnp.where((lane % 32) < 16, pltpu.roll(xn, LANES - 16, 1), pltpu.roll(xn, 16, 1))
    return xn * cos + partner * sin


def _ffn_in_kernel(x_ref, cos_ref, sin_ref, n1pre_ref, wg_ref, wu_ref, wd_ref, n1post_ref,
                   mixpre_ref, win_ref, qn_ref, kn_ref,
                   h1_ref, qa_ref, ka_ref, vat_ref, qb_ref, kb_ref, vbt_ref, act_ref,
                   *, na_chunk):
    tm = x_ref.shape[0]
    h1 = _swiglu_half(x_ref[...], n1pre_ref[...], wg_ref, wu_ref, wd_ref, n1post_ref[...], act_ref)
    h1_ref[...] = h1
    hn = _rms(h1, mixpre_ref[...]).astype(bf16)

    def proj(c0, width):
        return jnp.dot(hn, win_ref[:, c0:c0 + width], preferred_element_type=f32)

    scale = HEAD_DIM ** -0.5
    qa_ref[...] = (proj(0, W_A) * (scale * LOG2_E)).astype(bf16)
    ka_ref[...] = proj(W_A, W_A).astype(bf16)
    va = proj(2 * W_A, W_A)
    for j in range(tm // na_chunk):
        vat_ref[j] = va[j * na_chunk:(j + 1) * na_chunk, :].T.astype(bf16)

    lane = lax.broadcasted_iota(jnp.int32, (tm, LANES), 1)
    cos = cos_ref[...]
    sin = sin_ref[...]
    qn = qn_ref[...]
    kn = kn_ref[...]
    for s in range(0, W_B, 2 * LANES):
        qpair = proj(3 * W_A + s, 2 * LANES)
        for c0 in (0, LANES):
            qb_ref[:, s + c0:s + c0 + LANES] = (
                _norm_rope_slab(qpair[:, c0:c0 + LANES], qn, cos, sin, lane) * (scale * LOG2_E)).astype(bf16)
    kv = proj(3 * W_A + W_B, 2 * W_KV)
    kb_ref[...] = _norm_rope_slab(kv[:, :W_KV], kn, cos, sin, lane).astype(bf16)
    vbt_ref[0] = kv[:, W_KV:].T.astype(bf16)


def _ffn_in(x2, cos, sin, p, *, tm, na_chunk, gqa_chunk):
    t, d = x2.shape
    s_len = cos.shape[0]
    assert t % tm == 0 and s_len % tm == 0 and tm % na_chunk == 0
    assert gqa_chunk % tm == 0 and t % gqa_chunk == 0
    n_pos = s_len // tm
    tiles_per_chunk = gqa_chunk // tm
    d_ff = p['wg1'].shape[1]
    in_cols = p['w_in'].shape[1]
    tok = lambda w: pl.BlockSpec((tm, w), lambda i: (i, 0))
    pos = pl.BlockSpec((tm, LANES), lambda i: (i % n_pos, 0))
    out_shape = (
        jax.ShapeDtypeStruct((t, d), f32),
        jax.ShapeDtypeStruct((t, W_A), bf16),
        jax.ShapeDtypeStruct((t, W_A), bf16),
        jax.ShapeDtypeStruct((t // na_chunk, W_A, na_chunk), bf16),
        jax.ShapeDtypeStruct((t, W_B), bf16),
        jax.ShapeDtypeStruct((t, W_KV), bf16),
        jax.ShapeDtypeStruct((t // gqa_chunk, W_KV, gqa_chunk), bf16),
    )
    out_specs = (
        tok(d), tok(W_A), tok(W_A),
        pl.BlockSpec((tm // na_chunk, W_A, na_chunk), lambda i: (i, 0, 0)),
        tok(W_B), tok(W_KV),
        pl.BlockSpec((1, W_KV, tm), lambda i: (i // tiles_per_chunk, 0, i % tiles_per_chunk)),
    )
    in_specs = [
        tok(d), pos, pos,
        _const_spec((1, d)), _const_spec((d, d_ff)), _const_spec((d, d_ff)), _const_spec((d_ff, d)),
        _const_spec((1, d)), _const_spec((1, d)), _const_spec((d, in_cols)),
        _const_spec((1, LANES)), _const_spec((1, LANES)),
    ]
    return pl.pallas_call(
        functools.partial(_ffn_in_kernel, na_chunk=na_chunk),
        grid=(t // tm,),
        in_specs=in_specs,
        out_specs=out_specs,
        out_shape=out_shape,
        scratch_shapes=[pltpu.VMEM((tm, d_ff), bf16)],
        compiler_params=pltpu.CompilerParams(dimension_semantics=("arbitrary",),
                                             vmem_limit_bytes=VMEM_LIMIT),
        name="ffn_in",
    )(x2, cos, sin, p['n1pre'], p['wg1'], p['wu1'], p['wd1'], p['n1post'],
      p['mixpre'], p['w_in'], p['qn'], p['kn'])


def _na_kernel(q_ref, k0_ref, k1_ref, k2_ref, v0_ref, v1_ref, v2_ref, km_ref, vm_ref,
               strips_ref, mb_ref, g_ref, o_ref, kcat_ref, vcat_ref, s_ref, ot_ref):
    j = pl.program_id(1)
    n_blk = pl.num_programs(1)
    blk0 = jnp.clip(j - 1, 0, n_blk - 3)
    n_keys = NA_KEY_ROWS * GRID_W
    for p, (k_ref, v_ref) in enumerate(((k0_ref, v0_ref), (k1_ref, v1_ref), (k2_ref, v2_ref))):
        kcat_ref[p * NA_BLOCK:(p + 1) * NA_BLOCK, :] = k_ref[...]
        vcat_ref[:, p * NA_BLOCK:(p + 1) * NA_BLOCK] = v_ref[0]
    kcat_ref[n_keys:, :] = km_ref[...]
    vcat_ref[:, n_keys:] = vm_ref[0]

    n_strips = strips_ref.shape[2]
    strip0 = NA_ROWS * (blk0 - j) + NA_WIN_H - 1 - STRIP_D0 + 1
    strip_idx = [jnp.clip(strip0 + a, 0, n_strips - 1) for a in range(NA_KEY_ROWS)]
    lane = lax.broadcasted_iota(jnp.int32, (NA_BLOCK, LANES), 1)

    def scores(h):
        sl = slice(LANES * (h // 2), LANES * (h // 2) + LANES)
        in_head = (lane < HEAD_DIM) if h % 2 == 0 else (lane >= HEAD_DIM)
        qs = q_ref[:, sl]
        qm = jnp.where(in_head, qs, jnp.zeros_like(qs))
        s = lax.dot_general(kcat_ref[:, sl], qm, _NT, preferred_element_type=f32)
        s = s + jnp.concatenate([strips_ref[0, h, i] for i in strip_idx] + [mb_ref[h]], axis=0)
        s_ref[h % 2] = s
        return jnp.max(s, axis=0, keepdims=True)

    m = scores(0)
    for h in range(NA_HEADS):
        m_next = scores(h + 1) if h + 1 < NA_HEADS else None
        hs = slice(HEAD_DIM * h, HEAD_DIM * (h + 1))
        p = jnp.exp2(s_ref[h % 2] - m)
        l = jnp.sum(p, axis=0, keepdims=True)
        acc = jnp.dot(vcat_ref[hs, :], p.astype(bf16), preferred_element_type=f32)
        ot_ref[hs, :] = acc / l
        m = m_next
    o_ref[...] = _rms(ot_ref[...].T, g_ref[...]).astype(bf16)


def _na_attn(qa, ka, vat, km, vmt, strips, mb, g, *, batch, n_rows):
    t = qa.shape[0]
    assert n_rows % NA_ROWS == 0
    n_blk = n_rows // NA_ROWS
    assert n_blk >= 3 and t == batch * n_blk * NA_BLOCK
    n_cat = NA_KEY_ROWS * GRID_W + META_PAD

    def kv_idx(p):
        return lambda b, j: b * n_blk + jnp.clip(j - 1, 0, n_blk - 3) + p

    k_specs = [pl.BlockSpec((NA_BLOCK, W_A), (lambda f: lambda b, j: (f(b, j), 0))(kv_idx(p))) for p in range(3)]
    v_specs = [pl.BlockSpec((1, W_A, NA_BLOCK), (lambda f: lambda b, j: (f(b, j), 0, 0))(kv_idx(p)))
               for p in range(3)]
    q_spec = pl.BlockSpec((NA_BLOCK, W_A), lambda b, j: (b * n_blk + j, 0))
    strip_spec = pl.BlockSpec((1,) + strips.shape[1:],
                              lambda b, j: (jnp.where(j == 0, 0, jnp.where(j == n_blk - 1, 2, 1)), 0, 0, 0, 0))
    return pl.pallas_call(
        _na_kernel,
        grid=(batch, n_blk),
        in_specs=[q_spec, *k_specs, *v_specs,
                  _const_spec(km.shape), _const_spec(vmt.shape), strip_spec,
                  _const_spec(mb.shape), _const_spec(g.shape)],
        out_specs=pl.BlockSpec((NA_BLOCK, W_A), lambda b, j: (b * n_blk + j, 0)),
        out_shape=jax.ShapeDtypeStruct((t, W_A), bf16),
        scratch_shapes=[pltpu.VMEM((n_cat, W_A), bf16), pltpu.VMEM((W_A, n_cat), bf16),
                        pltpu.VMEM((2, n_cat, NA_BLOCK), f32), pltpu.VMEM((W_A, NA_BLOCK), f32)],
        compiler_params=pltpu.CompilerParams(dimension_semantics=("arbitrary", "arbitrary"),
                                             vmem_limit_bytes=VMEM_LIMIT),
        name="na_attn",
    )(qa, ka, ka, ka, vat, vat, vat, km, vmt, strips, mb, g)


def _loop(trips, body, init):
    return body(0, init) if trips == 1 else lax.fori_loop(0, trips, body, init)


def _chunk_rows(c, size):
    start = c * size
    return pl.ds(start if isinstance(c, int) else pl.multiple_of(start, size), size)


def _gqa_kernel(q_ref, k_ref, vt_ref, km_ref, vmt_ref, g_ref, o_ref, s_ref, ot_ref):
    tq = q_ref.shape[0]
    n_chunks = vt_ref.shape[0]
    kc = vt_ref.shape[2]
    lane = lax.broadcasted_iota(jnp.int32, (tq, LANES), 1)
    meta_row = lax.broadcasted_iota(jnp.int32, (META_PAD, tq), 0)

    def head_query(h):
        n = h // GQA_GROUP
        qs = q_ref[:, LANES * (h // 2):LANES * (h // 2) + LANES].astype(f32)
        if h % 2 != n:
            qs = pltpu.roll(qs, HEAD_DIM, 1)
        on_kv = (lane < HEAD_DIM) if n == 0 else (lane >= HEAD_DIM)
        return jnp.where(on_kv, qs, 0.0).astype(bf16)

    def meta_scores(qm):
        sm = lax.dot_general(km_ref[...], qm, _NT, preferred_element_type=f32)
        return jnp.where(meta_row < N_META, sm, NEG)

    def scores_chunk(h, qm, c, mx):
        rows = _chunk_rows(c, kc)
        s = lax.dot_general(k_ref[rows, :], qm, _NT, preferred_element_type=f32)
        s_ref[h % 2, rows, :] = s
        return jnp.maximum(mx, jnp.max(s, axis=0, keepdims=True))

    def weighted_chunk(h, m, c, l, acc):
        vs = slice(HEAD_DIM * (h // GQA_GROUP), HEAD_DIM * (h // GQA_GROUP + 1))
        p = jnp.exp2(s_ref[h % 2, _chunk_rows(c, kc), :] - m)
        l = l + jnp.sum(p, axis=0, keepdims=True)
        acc = acc + jnp.dot(vt_ref[c, vs, :], p.astype(bf16), preferred_element_type=f32)
        return l, acc

    qm = head_query(0)
    sm = meta_scores(qm)
    m = _loop(n_chunks, functools.partial(scores_chunk, 0, qm), jnp.max(sm, axis=0, keepdims=True))
    for h in range(GQA_HEADS):
        vs = slice(HEAD_DIM * (h // GQA_GROUP), HEAD_DIM * (h // GQA_GROUP + 1))
        pm = jnp.exp2(sm - m)
        l = jnp.sum(pm, axis=0, keepdims=True)
        acc = jnp.dot(vmt_ref[0, vs, :], pm.astype(bf16), preferred_element_type=f32)
        if h + 1 < GQA_HEADS:
            qm_next = head_query(h + 1)
            sm_next = meta_scores(qm_next)

            def both(c, carry, h=h, m=m, qm_next=qm_next):
                mx, l, acc = carry
                mx = scores_chunk(h + 1, qm_next, c, mx)
                l, acc = weighted_chunk(h, m, c, l, acc)
                return mx, l, acc

            m_next, l, acc = _loop(n_chunks, both, (jnp.max(sm_next, axis=0, keepdims=True), l, acc))
        else:
            l, acc = _loop(n_chunks, lambda c, carry, h=h, m=m: weighted_chunk(h, m, c, *carry), (l, acc))
        ot_ref[HEAD_DIM * h:HEAD_DIM * (h + 1), :] = acc / l
        if h + 1 < GQA_HEADS:
            sm, m = sm_next, m_next
    o_ref[...] = _rms(ot_ref[...].T, g_ref[...]).astype(bf16)


def _gqa_attn(qb, kb, vbt, km, vmt, g, *, batch, seq):
    t = qb.shape[0]
    tq = GQA_Q_TILE
    kc = vbt.shape[2]
    assert t == batch * seq and seq % tq == 0 and seq % kc == 0
    n_q = seq // tq
    return pl.pallas_call(
        _gqa_kernel,
        grid=(batch, n_q),
        in_specs=[pl.BlockSpec((tq, W_B), lambda b, i: (b * n_q + i, 0)),
                  pl.BlockSpec((seq, W_KV), lambda b, i: (b, 0), pipeline_mode=pl.Buffered(1)),
                  pl.BlockSpec((seq // kc, W_KV, kc), lambda b, i: (b, 0, 0), pipeline_mode=pl.Buffered(1)),
                  _const_spec(km.shape), _const_spec(vmt.shape), _const_spec(g.shape)],
        out_specs=pl.BlockSpec((tq, W_B), lambda b, i: (b * n_q + i, 0)),
        out_shape=jax.ShapeDtypeStruct((t, W_B), bf16),
        scratch_shapes=[pltpu.VMEM((2, seq, tq), f32), pltpu.VMEM((W_B, tq), f32)],
        compiler_params=pltpu.CompilerParams(dimension_semantics=("arbitrary", "arbitrary"),
                                             vmem_limit_bytes=VMEM_LIMIT),
        name="gqa_attn",
    )(qb, kb, vbt, km, vmt, g)


def _out_ffn_kernel(h1_ref, ya_ref, yb_ref, woa_ref, wob_ref, mixpost_ref,
                    n2pre_ref, wg_ref, wu_ref, wd_ref, n2post_ref, fin_ref, o_ref, act_ref):
    y = (jnp.dot(ya_ref[...], woa_ref[...], preferred_element_type=f32)
         + jnp.dot(yb_ref[...], wob_ref[...], preferred_element_type=f32))
    h2 = h1_ref[...] + _rms(y, mixpost_ref[...])
    h3 = _swiglu_half(h2, n2pre_ref[...], wg_ref, wu_ref, wd_ref, n2post_ref[...], act_ref)
    o_ref[...] = _rms(h3, fin_ref[...])


def _out_ffn(h1, ya, yb, p, *, tm):
    t, d = h1.shape
    d_ff = p['wg2'].shape[1]
    tok = lambda w: pl.BlockSpec((tm, w), lambda i: (i, 0))
    return pl.pallas_call(
        _out_ffn_kernel,
        grid=(t // tm,),
        in_specs=[tok(d), tok(W_A), tok(W_B),
                  _const_spec((W_A, d)), _const_spec((W_B, d)), _const_spec((1, d)),
                  _const_spec((1, d)), _const_spec((d, d_ff)), _const_spec((d, d_ff)),
                  _const_spec((d_ff, d)), _const_spec((1, d)), _const_spec((1, d))],
        out_specs=tok(d),
        out_shape=jax.ShapeDtypeStruct((t, d), f32),
        scratch_shapes=[pltpu.VMEM((tm, d_ff), bf16)],
        compiler_params=pltpu.CompilerParams(dimension_semantics=("arbitrary",),
                                             vmem_limit_bytes=VMEM_LIMIT),
        name="out_ffn",
    )(h1, ya, yb, p['woa'], p['wob'], p['mixpost'], p['n2pre'], p['wg2'], p['wu2'], p['wd2'],
      p['n2post'], p['fin'])


def _rope_tables(pos_row, pos_col):
    inv_freq = jnp.asarray(ROPE_THETA ** (-np.arange(0, ROPE_AXIS_DIM, 2) / ROPE_AXIS_DIM), f32)
    ang_r = pos_row[:, None] * inv_freq[None, :]
    ang_c = pos_col[:, None] * inv_freq[None, :]
    cos = jnp.concatenate([jnp.cos(ang_r)] * 2 + [jnp.cos(ang_c)] * 2, axis=-1)
    sin = jnp.concatenate([-jnp.sin(ang_r), jnp.sin(ang_r), -jnp.sin(ang_c), jnp.sin(ang_c)], axis=-1)
    return jnp.tile(cos, (1, 2)), jnp.tile(sin, (1, 2))


def _bias_strips(rel_bias):
    n_h, n_d, n_k = rel_bias.shape
    assert n_d == 2 * NA_WIN_H - 1 and n_k == 2 * NA_WIN_W - 1
    period = 2 * GRID_W - 1
    lead = GRID_W - NA_WIN_W
    z = jnp.concatenate([jnp.zeros((n_h, n_d, lead), f32), rel_bias.astype(f32)[..., ::-1],
                         jnp.zeros((n_h, n_d, period - lead - n_k), f32)], axis=-1)
    hank = jnp.tile(z, (1, 1, GRID_W + 1))[..., :GRID_W * (period + 1)]
    toep = hank.reshape(n_h, n_d, GRID_W, period + 1)[..., ::-1, :GRID_W]
    c = np.arange(GRID_W)[:, None]
    q = np.arange(GRID_W)[None, :]
    col_start = np.clip(q - NA_WIN_W // 2, 0, GRID_W - NA_WIN_W)
    toep = jnp.where((c >= col_start) & (c < col_start + NA_WIN_W), toep * LOG2_E, NEG)
    n_strips = n_d - STRIP_D0
    strips = jnp.concatenate([toep[:, STRIP_D0 - b:STRIP_D0 - b + n_strips] for b in range(NA_ROWS)], axis=-1)
    b = np.arange(NA_ROWS)[None, :]
    rel = np.arange(n_strips)[:, None] + STRIP_D0 - b - (NA_WIN_H - 1)
    half = NA_WIN_H // 2
    row_ok = np.stack([(rel >= -b) & (rel < NA_WIN_H - b),
                       (rel >= -half) & (rel < half),
                       (rel >= -half - b) & (rel < half - b)])
    row_ok = np.repeat(row_ok, GRID_W, axis=-1)[:, None, :, None, :]
    strips = jnp.where(row_ok, strips[None], NEG)
    return jnp.pad(strips, ((0, 0), (0, 0), (1, 1), (0, 0), (0, 0)), constant_values=NEG)


def kernel(x_prompt, x_sample, meta_tokens, ffn1_norm_pre, ffn1_w_gate, ffn1_w_up, ffn1_w_down, ffn1_norm_post, mix_norm_pre, w_in, na_rel_bias, na_meta_bias, gqa_q_norm, gqa_k_norm, grp_norm_a, grp_norm_b, w_out, mix_norm_post, ffn2_norm_pre, ffn2_w_gate, ffn2_w_up, ffn2_w_down, ffn2_norm_post, final_norm):
    assert ffn1_w_gate.shape[0] == 1, "meta keys/values are shared across the batch only for depth 1"
    d = x_prompt.shape[-1]
    row = lambda v: v.reshape(1, -1).astype(f32)
    p = dict(
        n1pre=row(ffn1_norm_pre[0]), wg1=ffn1_w_gate[0].astype(bf16), wu1=ffn1_w_up[0].astype(bf16),
        wd1=ffn1_w_down[0].astype(bf16), n1post=row(ffn1_norm_post[0]),
        mixpre=row(mix_norm_pre[0]), w_in=w_in[0].astype(bf16),
        qn=row(jnp.tile(gqa_q_norm[0], 2)), kn=row(jnp.tile(gqa_k_norm[0], 2)),
        woa=w_out[0, :W_A].astype(bf16), wob=w_out[0, W_A:].astype(bf16), mixpost=row(mix_norm_post[0]),
        n2pre=row(ffn2_norm_pre[0]), wg2=ffn2_w_gate[0].astype(bf16), wu2=ffn2_w_up[0].astype(bf16),
        wd2=ffn2_w_down[0].astype(bf16), n2post=row(ffn2_norm_post[0]), fin=row(final_norm),
    )
    ga = row(grp_norm_a[0])
    gb = row(grp_norm_b[0])
    strips = _bias_strips(na_rel_bias[0])
    mb = jnp.where(np.arange(META_PAD)[None, :, None] < N_META,
                   jnp.pad(na_meta_bias[0].astype(f32) * LOG2_E, ((0, 0), (0, META_PAD - N_META)))[:, :, None], NEG)
    mb = jnp.broadcast_to(mb, (NA_HEADS, META_PAD, NA_BLOCK))

    xm = jnp.pad(meta_tokens.astype(f32), ((0, META_PAD - N_META), (0, 0)))
    mcos, msin = _rope_tables(jnp.pad(-jnp.ones((N_META,), f32), (0, META_PAD - N_META)),
                              jnp.pad(jnp.arange(N_META, dtype=f32), (0, META_PAD - N_META)))
    _, _, kam, vamt, _, kbm, vbmt = _ffn_in(xm, mcos, msin, p, tm=META_PAD, na_chunk=META_PAD,
                                            gqa_chunk=META_PAD)

    def run(x):
        batch, seq, _ = x.shape
        n_rows = seq // GRID_W
        tpos = jnp.arange(seq)
        cos, sin = _rope_tables((tpos // GRID_W).astype(f32), (tpos % GRID_W).astype(f32))
        h1, qa, ka, vat, qb, kb, vbt = _ffn_in(x.reshape(batch * seq, d), cos, sin, p, tm=TOKEN_TILE,
                                               na_chunk=NA_BLOCK,
                                               gqa_chunk=GQA_KEY_CHUNK if seq % GQA_KEY_CHUNK == 0 else seq)
        ya = _na_attn(qa, ka, vat, kam, vamt, strips, mb, ga, batch=batch, n_rows=n_rows)
        yb = _gqa_attn(qb, kb, vbt, kbm, vbmt, gb, batch=batch, seq=seq)
        return _out_ffn(h1, ya, yb, p, tm=TOKEN_TILE).reshape(batch, seq, d)

    return (run(x_prompt), run(x_sample))
```

```python
import functools

import numpy as np
import jax
import jax.numpy as jnp
from jax import lax
from jax.experimental import pallas as pl
from jax.experimental.pallas import tpu as pltpu

N_META = 16
GRID_W = 64
HEAD_DIM = 64
NA_HEADS = 8
NA_WIN_H = 8
NA_WIN_W = 16
GQA_HEADS = 8
GQA_KV_HEADS = 2
GQA_GROUP = GQA_HEADS // GQA_KV_HEADS
W_A = NA_HEADS * HEAD_DIM
W_B = GQA_HEADS * HEAD_DIM
W_KV = GQA_KV_HEADS * HEAD_DIM
ROPE_THETA = 10000.0
ROPE_AXIS_DIM = HEAD_DIM // 2
EPS = 1e-6

LANES = 128
NEG = -1e30
LOG2_E = 1.4426950408889634
META_PAD = 128
FF_CHUNK = 256
MIN_HALF_TILE = 256
TOKEN_TILE = 512
NA_ROWS = 4
NA_BLOCK = NA_ROWS * GRID_W
NA_KEY_ROWS = 3 * NA_ROWS
assert NA_ROWS == NA_WIN_H // 2
STRIP_D0 = NA_ROWS - 1
GQA_Q_TILE = 256
GQA_KEY_CHUNK = 4096
VMEM_LIMIT = 56 * 1024 * 1024

_NT = (((1,), (1,)), ((), ()))

f32 = jnp.float32
bf16 = jnp.bfloat16


def _rms(x, g):
    ms = jnp.mean(x * x, axis=-1, keepdims=True)
    return x * lax.rsqrt(ms + EPS) * g


def _const_spec(shape):
    nd = len(shape)
    return pl.BlockSpec(shape, lambda *_: (0,) * nd, pipeline_mode=pl.Buffered(1))


def _row_halves(tm):
    if tm < 2 * MIN_HALF_TILE:
        return (slice(0, tm),)
    return (slice(0, tm // 2), slice(tm // 2, tm))


def _swiglu_act(x, npre, wg_ref, wu_ref, act_ref):
    xn = _rms(x, npre).astype(bf16)
    d_ff = wg_ref.shape[1]
    for c0 in range(0, d_ff, FF_CHUNK):
        g = jnp.dot(xn, wg_ref[:, c0:c0 + FF_CHUNK], preferred_element_type=f32)
        u = jnp.dot(xn, wu_ref[:, c0:c0 + FF_CHUNK], preferred_element_type=f32)
        act_ref[:, c0:c0 + FF_CHUNK] = (g * jax.nn.sigmoid(g) * u).astype(bf16)


def _norm_rope_slab(x, g, cos, sin, lane):
    sq = x * x
    lo = jnp.sum(jnp.where(lane < HEAD_DIM, sq, 0.0), axis=-1, keepdims=True)
    hi = jnp.sum(jnp.where(lane < HEAD_DIM, 0.0, sq), axis=-1, keepdims=True)
    ms = jnp.where(lane < HEAD_DIM, lo, hi) * (1.0 / HEAD_DIM)
    xn = x * lax.rsqrt(ms + EPS) * g
    partner = jnp.where((lane % 32) < 16, pltpu.roll(xn, LANES - 16, 1), pltpu.roll(xn, 16, 1))
    return xn * cos + partner * sin


def _ffn_in_kernel(x_ref, cos_ref, sin_ref, n1pre_ref, wg_ref, wu_ref, wd_ref, n1post_ref,
                   mixpre_ref, win_ref, qn_ref, kn_ref,
                   h1_ref, qa_ref, ka_ref, vat_ref, qb_ref, kb_ref, vbt_ref, act_ref,
                   *, na_chunk):
    tm = x_ref.shape[0]
    halves = _row_halves(tm)
    _swiglu_act(x_ref[...], n1pre_ref[...], wg_ref, wu_ref, act_ref)
    ys = [jnp.dot(act_ref[r, :], wd_ref[...], preferred_element_type=f32) for r in halves]
    qscale = HEAD_DIM ** -0.5 * LOG2_E
    qn = qn_ref[...]
    kn = kn_ref[...]
    for r, y in zip(halves, ys):
        n = r.stop - r.start
        assert n % na_chunk == 0
        h1 = x_ref[r, :] + 0.5 * _rms(y, n1post_ref[...])
        h1_ref[r, :] = h1
        hn = _rms(h1, mixpre_ref[...]).astype(bf16)

        def proj(c0, width, hn=hn):
            return jnp.dot(hn, win_ref[:, c0:c0 + width], preferred_element_type=f32)

        lane = lax.broadcasted_iota(jnp.int32, (n, LANES), 1)
        cos = cos_ref[r, :]
        sin = sin_ref[r, :]
        kv = proj(3 * W_A + W_B, 2 * W_KV)
        kb_ref[r, :] = _norm_rope_slab(kv[:, :W_KV], kn, cos, sin, lane).astype(bf16)
        vbt_ref[0, :, r] = kv[:, W_KV:].T.astype(bf16)
        for s in range(0, W_B, 2 * LANES):
            qpair = proj(3 * W_A + s, 2 * LANES)
            for c0 in (0, LANES):
                qb_ref[r, s + c0:s + c0 + LANES] = (
                    _norm_rope_slab(qpair[:, c0:c0 + LANES], qn, cos, sin, lane) * qscale).astype(bf16)
        va = proj(2 * W_A, W_A)
        for j in range(n // na_chunk):
            vat_ref[r.start // na_chunk + j] = va[j * na_chunk:(j + 1) * na_chunk, :].T.astype(bf16)
        ka_ref[r, :] = proj(W_A, W_A).astype(bf16)
        qa_ref[r, :] = (proj(0, W_A) * qscale).astype(bf16)


def _ffn_in(x2, cos, sin, p, *, tm, na_chunk, gqa_chunk):
    t, d = x2.shape
    s_len = cos.shape[0]
    assert t % tm == 0 and s_len % tm == 0 and tm % na_chunk == 0
    assert gqa_chunk % tm == 0 and t % gqa_chunk == 0
    n_pos = s_len // tm
    tiles_per_chunk = gqa_chunk // tm
    d_ff = p['wg1'].shape[1]
    in_cols = p['w_in'].shape[1]
    tok = lambda w: pl.BlockSpec((tm, w), lambda i: (i, 0))
    pos = pl.BlockSpec((tm, LANES), lambda i: (i % n_pos, 0))
    out_shape = (
        jax.ShapeDtypeStruct((t, d), f32),
        jax.ShapeDtypeStruct((t, W_A), bf16),
        jax.ShapeDtypeStruct((t, W_A), bf16),
        jax.ShapeDtypeStruct((t // na_chunk, W_A, na_chunk), bf16),
        jax.ShapeDtypeStruct((t, W_B), bf16),
        jax.ShapeDtypeStruct((t, W_KV), bf16),
        jax.ShapeDtypeStruct((t // gqa_chunk, W_KV, gqa_chunk), bf16),
    )
    out_specs = (
        tok(d), tok(W_A), tok(W_A),
        pl.BlockSpec((tm // na_chunk, W_A, na_chunk), lambda i: (i, 0, 0)),
        tok(W_B), tok(W_KV),
        pl.BlockSpec((1, W_KV, tm), lambda i: (i // tiles_per_chunk, 0, i % tiles_per_chunk)),
    )
    in_specs = [
        tok(d), pos, pos,
        _const_spec((1, d)), _const_spec((d, d_ff)), _const_spec((d, d_ff)), _const_spec((d_ff, d)),
        _const_spec((1, d)), _const_spec((1, d)), _const_spec((d, in_cols)),
        _const_spec((1, LANES)), _const_spec((1, LANES)),
    ]
    return pl.pallas_call(
        functools.partial(_ffn_in_kernel, na_chunk=na_chunk),
        grid=(t // tm,),
        in_specs=in_specs,
        out_specs=out_specs,
        out_shape=out_shape,
        scratch_shapes=[pltpu.VMEM((tm, d_ff), bf16)],
        compiler_params=pltpu.CompilerParams(dimension_semantics=("arbitrary",),
                                             vmem_limit_bytes=VMEM_LIMIT),
        name="ffn_in",
    )(x2, cos, sin, p['n1pre'], p['wg1'], p['wu1'], p['wd1'], p['n1post'],
      p['mixpre'], p['w_in'], p['qn'], p['kn'])


def _na_kernel(q_ref, k0_ref, k1_ref, k2_ref, v0_ref, v1_ref, v2_ref, km_ref, vm_ref,
               strips_ref, mb_ref, g_ref, o_ref, kcat_ref, vcat_ref, s_ref, ot_ref):
    j = pl.program_id(1)
    n_blk = pl.num_programs(1)
    blk0 = jnp.clip(j - 1, 0, n_blk - 3)
    n_keys = NA_KEY_ROWS * GRID_W
    for p, (k_ref, v_ref) in enumerate(((k0_ref, v0_ref), (k1_ref, v1_ref), (k2_ref, v2_ref))):
        kcat_ref[p * NA_BLOCK:(p + 1) * NA_BLOCK, :] = k_ref[...]
        vcat_ref[:, p * NA_BLOCK:(p + 1) * NA_BLOCK] = v_ref[0]
    kcat_ref[n_keys:, :] = km_ref[...]
    vcat_ref[:, n_keys:] = vm_ref[0]

    n_strips = strips_ref.shape[2]
    strip0 = NA_ROWS * (blk0 - j) + NA_WIN_H - 1 - STRIP_D0 + 1
    strip_idx = [jnp.clip(strip0 + a, 0, n_strips - 1) for a in range(NA_KEY_ROWS)]
    lane = lax.broadcasted_iota(jnp.int32, (NA_BLOCK, LANES), 1)

    def scores(h):
        sl = slice(LANES * (h // 2), LANES * (h // 2) + LANES)
        in_head = (lane < HEAD_DIM) if h % 2 == 0 else (lane >= HEAD_DIM)
        qs = q_ref[:, sl]
        qm = jnp.where(in_head, qs, jnp.zeros_like(qs))
        s = lax.dot_general(kcat_ref[:, sl], qm, _NT, preferred_element_type=f32)
        s = s + jnp.concatenate([strips_ref[0, h, i] for i in strip_idx] + [mb_ref[h]], axis=0)
        s_ref[h % 2] = s
        return jnp.max(s, axis=0, keepdims=True)

    m = scores(0)
    for h in range(NA_HEADS):
        m_next = scores(h + 1) if h + 1 < NA_HEADS else None
        hs = slice(HEAD_DIM * h, HEAD_DIM * (h + 1))
        p = jnp.exp2(s_ref[h % 2] - m)
        l = jnp.sum(p, axis=0, keepdims=True)
        acc = jnp.dot(vcat_ref[hs, :], p.astype(bf16), preferred_element_type=f32)
        ot_ref[hs, :] = acc / l
        m = m_next
    o_ref[...] = _rms(ot_ref[...].T, g_ref[...]).astype(bf16)


def _na_attn(qa, ka, vat, km, vmt, strips, mb, g, *, batch, n_rows):
    t = qa.shape[0]
    assert n_rows % NA_ROWS == 0
    n_blk = n_rows // NA_ROWS
    assert n_blk >= 3 and t == batch * n_blk * NA_BLOCK
    n_cat = NA_KEY_ROWS * GRID_W + META_PAD

    def kv_idx(p):
        return lambda b, j: b * n_blk + jnp.clip(j - 1, 0, n_blk - 3) + p

    k_specs = [pl.BlockSpec((NA_BLOCK, W_A), (lambda f: lambda b, j: (f(b, j), 0))(kv_idx(p))) for p in range(3)]
    v_specs = [pl.BlockSpec((1, W_A, NA_BLOCK), (lambda f: lambda b, j: (f(b, j), 0, 0))(kv_idx(p)))
               for p in range(3)]
    q_spec = pl.BlockSpec((NA_BLOCK, W_A), lambda b, j: (b * n_blk + j, 0))
    strip_spec = pl.BlockSpec((1,) + strips.shape[1:],
                              lambda b, j: (jnp.where(j == 0, 0, jnp.where(j == n_blk - 1, 2, 1)), 0, 0, 0, 0))
    return pl.pallas_call(
        _na_kernel,
        grid=(batch, n_blk),
        in_specs=[q_spec, *k_specs, *v_specs,
                  _const_spec(km.shape), _const_spec(vmt.shape), strip_spec,
                  _const_spec(mb.shape), _const_spec(g.shape)],
        out_specs=pl.BlockSpec((NA_BLOCK, W_A), lambda b, j: (b * n_blk + j, 0)),
        out_shape=jax.ShapeDtypeStruct((t, W_A), bf16),
        scratch_shapes=[pltpu.VMEM((n_cat, W_A), bf16), pltpu.VMEM((W_A, n_cat), bf16),
                        pltpu.VMEM((2, n_cat, NA_BLOCK), f32), pltpu.VMEM((W_A, NA_BLOCK), f32)],
        compiler_params=pltpu.CompilerParams(dimension_semantics=("arbitrary", "arbitrary"),
                                             vmem_limit_bytes=VMEM_LIMIT),
        name="na_attn",
    )(qa, ka, ka, ka, vat, vat, vat, km, vmt, strips, mb, g)


def _loop(trips, body, init):
    return body(0, init) if trips == 1 else lax.fori_loop(0, trips, body, init)


def _chunk_rows(c, size):
    start = c * size
    return pl.ds(start if isinstance(c, int) else pl.multiple_of(start, size), size)


def _gqa_kernel(q_ref, q_next_ref, k_ref, vt_ref, km_ref, vmt_ref, g_ref, o_ref,
                s_ref, ot_ref, sm_ref, m_ref):
    assert GQA_HEADS % 2 == 0
    tq = q_ref.shape[0]
    n_chunks = vt_ref.shape[0]
    kc = vt_ref.shape[2]
    lane = lax.broadcasted_iota(jnp.int32, (tq, LANES), 1)
    meta_row = lax.broadcasted_iota(jnp.int32, (META_PAD, tq), 0)

    def head_query(q_ref, h):
        n = h // GQA_GROUP
        qs = q_ref[:, LANES * (h // 2):LANES * (h // 2) + LANES].astype(f32)
        if h % 2 != n:
            qs = pltpu.roll(qs, HEAD_DIM, 1)
        on_kv = (lane < HEAD_DIM) if n == 0 else (lane >= HEAD_DIM)
        return jnp.where(on_kv, qs, 0.0).astype(bf16)

    def meta_scores(qm):
        sm = lax.dot_general(km_ref[...], qm, _NT, preferred_element_type=f32)
        return jnp.where(meta_row < N_META, sm, NEG)

    def scores_chunk(h, qm, c, mx):
        rows = _chunk_rows(c, kc)
        s = lax.dot_general(k_ref[rows, :], qm, _NT, preferred_element_type=f32)
        s_ref[h % 2, rows, :] = s
        return jnp.maximum(mx, jnp.max(s, axis=0, keepdims=True))

    def weighted_chunk(h, m, c, l, acc):
        vs = slice(HEAD_DIM * (h // GQA_GROUP), HEAD_DIM * (h // GQA_GROUP + 1))
        p = jnp.exp2(s_ref[h % 2, _chunk_rows(c, kc), :] - m)
        l = l + jnp.sum(p, axis=0, keepdims=True)
        acc = acc + jnp.dot(vt_ref[c, vs, :], p.astype(bf16), preferred_element_type=f32)
        return l, acc

    @pl.when(pl.program_id(1) == 0)
    def _():
        qm = head_query(q_ref, 0)
        sm = meta_scores(qm)
        sm_ref[...] = sm
        m_ref[...] = _loop(n_chunks, functools.partial(scores_chunk, 0, qm), jnp.max(sm, axis=0, keepdims=True))

    sm = sm_ref[...]
    m = m_ref[...]
    for h in range(GQA_HEADS):
        vs = slice(HEAD_DIM * (h // GQA_GROUP), HEAD_DIM * (h // GQA_GROUP + 1))
        pm = jnp.exp2(sm - m)
        l = jnp.sum(pm, axis=0, keepdims=True)
        acc = jnp.dot(vmt_ref[0, vs, :], pm.astype(bf16), preferred_element_type=f32)
        qm_next = head_query(q_ref, h + 1) if h + 1 < GQA_HEADS else head_query(q_next_ref, 0)
        sm_next = meta_scores(qm_next)

        def both(c, carry, h=h, m=m, qm_next=qm_next):
            mx, l, acc = carry
            mx = scores_chunk(h + 1, qm_next, c, mx)
            l, acc = weighted_chunk(h, m, c, l, acc)
            return mx, l, acc

        m_next, l, acc = _loop(n_chunks, both, (jnp.max(sm_next, axis=0, keepdims=True), l, acc))
        ot_ref[HEAD_DIM * h:HEAD_DIM * (h + 1), :] = acc / l
        sm, m = sm_next, m_next
    sm_ref[...] = sm
    m_ref[...] = m
    o_ref[...] = _rms(ot_ref[...].T, g_ref[...]).astype(bf16)


def _gqa_attn(qb, kb, vbt, km, vmt, g, *, batch, seq):
    t = qb.shape[0]
    tq = GQA_Q_TILE
    kc = vbt.shape[2]
    assert t == batch * seq and seq % tq == 0 and seq % kc == 0
    n_q = seq // tq
    return pl.pallas_call(
        _gqa_kernel,
        grid=(batch, n_q),
        in_specs=[pl.BlockSpec((tq, W_B), lambda b, i: (b * n_q + i, 0)),
                  pl.BlockSpec((tq, W_B), lambda b, i: (b * n_q + jnp.minimum(i + 1, n_q - 1), 0)),
                  pl.BlockSpec((seq, W_KV), lambda b, i: (b, 0), pipeline_mode=pl.Buffered(1)),
                  pl.BlockSpec((seq // kc, W_KV, kc), lambda b, i: (b, 0, 0), pipeline_mode=pl.Buffered(1)),
                  _const_spec(km.shape), _const_spec(vmt.shape), _const_spec(g.shape)],
        out_specs=pl.BlockSpec((tq, W_B), lambda b, i: (b * n_q + i, 0)),
        out_shape=jax.ShapeDtypeStruct((t, W_B), bf16),
        scratch_shapes=[pltpu.VMEM((2, seq, tq), f32), pltpu.VMEM((W_B, tq), f32),
                        pltpu.VMEM((META_PAD, tq), f32), pltpu.VMEM((1, tq), f32)],
        compiler_params=pltpu.CompilerParams(dimension_semantics=("arbitrary", "arbitrary"),
                                             vmem_limit_bytes=VMEM_LIMIT),
        name="gqa_attn",
    )(qb, qb, kb, vbt, km, vmt, g)


def _out_ffn_kernel(h1_ref, ya_ref, yb_ref, woa_ref, wob_ref, mixpost_ref,
                    n2pre_ref, wg_ref, wu_ref, wd_ref, n2post_ref, fin_ref, o_ref, act_ref, h2_ref):
    halves = _row_halves(h1_ref.shape[0])
    ys = [jnp.dot(ya_ref[r, :], woa_ref[...], preferred_element_type=f32)
          + jnp.dot(yb_ref[r, :], wob_ref[...], preferred_element_type=f32) for r in halves]
    for r, y in zip(halves, ys):
        h2_ref[r, :] = h1_ref[r, :] + _rms(y, mixpost_ref[...])
    for r in halves:
        _swiglu_act(h2_ref[r, :], n2pre_ref[...], wg_ref, wu_ref, act_ref.at[r, :])
    ys = [jnp.dot(act_ref[r, :], wd_ref[...], preferred_element_type=f32) for r in halves]
    for r, y in zip(halves, ys):
        h3 = h2_ref[r, :] + 0.5 * _rms(y, n2post_ref[...])
        o_ref[r, :] = _rms(h3, fin_ref[...])


def _out_ffn(h1, ya, yb, p, *, tm):
    t, d = h1.shape
    d_ff = p['wg2'].shape[1]
    tok = lambda w: pl.BlockSpec((tm, w), lambda i: (i, 0))
    return pl.pallas_call(
        _out_ffn_kernel,
        grid=(t // tm,),
        in_specs=[tok(d), tok(W_A), tok(W_B),
                  _const_spec((W_A, d)), _const_spec((W_B, d)), _const_spec((1, d)),
                  _const_spec((1, d)), _const_spec((d, d_ff)), _const_spec((d, d_ff)),
                  _const_spec((d_ff, d)), _const_spec((1, d)), _const_spec((1, d))],
        out_specs=tok(d),
        out_shape=jax.ShapeDtypeStruct((t, d), f32),
        scratch_shapes=[pltpu.VMEM((tm, d_ff), bf16), pltpu.VMEM((tm, d), f32)],
        compiler_params=pltpu.CompilerParams(dimension_semantics=("arbitrary",),
                                             vmem_limit_bytes=VMEM_LIMIT),
        name="out_ffn",
    )(h1, ya, yb, p['woa'], p['wob'], p['mixpost'], p['n2pre'], p['wg2'], p['wu2'], p['wd2'],
      p['n2post'], p['fin'])


def _rope_tables(pos_row, pos_col):
    inv_freq = jnp.asarray(ROPE_THETA ** (-np.arange(0, ROPE_AXIS_DIM, 2) / ROPE_AXIS_DIM), f32)
    ang_r = pos_row[:, None] * inv_freq[None, :]
    ang_c = pos_col[:, None] * inv_freq[None, :]
    cos = jnp.concatenate([jnp.cos(ang_r)] * 2 + [jnp.cos(ang_c)] * 2, axis=-1)
    sin = jnp.concatenate([-jnp.sin(ang_r), jnp.sin(ang_r), -jnp.sin(ang_c), jnp.sin(ang_c)], axis=-1)
    return jnp.tile(cos, (1, 2)), jnp.tile(sin, (1, 2))


def _bias_strips(rel_bias):
    n_h, n_d, n_k = rel_bias.shape
    assert n_d == 2 * NA_WIN_H - 1 and n_k == 2 * NA_WIN_W - 1
    period = 2 * GRID_W - 1
    lead = GRID_W - NA_WIN_W
    z = jnp.concatenate([jnp.zeros((n_h, n_d, lead), f32), rel_bias.astype(f32)[..., ::-1],
                         jnp.zeros((n_h, n_d, period - lead - n_k), f32)], axis=-1)
    hank = jnp.tile(z, (1, 1, GRID_W + 1))[..., :GRID_W * (period + 1)]
    toep = hank.reshape(n_h, n_d, GRID_W, period + 1)[..., ::-1, :GRID_W]
    c = np.arange(GRID_W)[:, None]
    q = np.arange(GRID_W)[None, :]
    col_start = np.clip(q - NA_WIN_W // 2, 0, GRID_W - NA_WIN_W)
    toep = jnp.where((c >= col_start) & (c < col_start + NA_WIN_W), toep * LOG2_E, NEG)
    n_strips = n_d - STRIP_D0
    strips = jnp.concatenate([toep[:, STRIP_D0 - b:STRIP_D0 - b + n_strips] for b in range(NA_ROWS)], axis=-1)
    b = np.arange(NA_ROWS)[None, :]
    rel = np.arange(n_strips)[:, None] + STRIP_D0 - b - (NA_WIN_H - 1)
    half = NA_WIN_H // 2
    row_ok = np.stack([(rel >= -b) & (rel < NA_WIN_H - b),
                       (rel >= -half) & (rel < half),
                       (rel >= -half - b) & (rel < half - b)])
    row_ok = np.repeat(row_ok, GRID_W, axis=-1)[:, None, :, None, :]
    strips = jnp.where(row_ok, strips[None], NEG)
    return jnp.pad(strips, ((0, 0), (0, 0), (1, 1), (0, 0), (0, 0)), constant_values=NEG)


def kernel(x_prompt, x_sample, meta_tokens, ffn1_norm_pre, ffn1_w_gate, ffn1_w_up, ffn1_w_down, ffn1_norm_post, mix_norm_pre, w_in, na_rel_bias, na_meta_bias, gqa_q_norm, gqa_k_norm, grp_norm_a, grp_norm_b, w_out, mix_norm_post, ffn2_norm_pre, ffn2_w_gate, ffn2_w_up, ffn2_w_down, ffn2_norm_post, final_norm):
    assert ffn1_w_gate.shape[0] == 1, "meta keys/values are shared across the batch only for depth 1"
    d = x_prompt.shape[-1]
    row = lambda v: v.reshape(1, -1).astype(f32)
    p = dict(
        n1pre=row(ffn1_norm_pre[0]), wg1=ffn1_w_gate[0].astype(bf16), wu1=ffn1_w_up[0].astype(bf16),
        wd1=ffn1_w_down[0].astype(bf16), n1post=row(ffn1_norm_post[0]),
        mixpre=row(mix_norm_pre[0]), w_in=w_in[0].astype(bf16),
        qn=row(jnp.tile(gqa_q_norm[0], 2)), kn=row(jnp.tile(gqa_k_norm[0], 2)),
        woa=w_out[0, :W_A].astype(bf16), wob=w_out[0, W_A:].astype(bf16), mixpost=row(mix_norm_post[0]),
        n2pre=row(ffn2_norm_pre[0]), wg2=ffn2_w_gate[0].astype(bf16), wu2=ffn2_w_up[0].astype(bf16),
        wd2=ffn2_w_down[0].astype(bf16), n2post=row(ffn2_norm_post[0]), fin=row(final_norm),
    )
    ga = row(grp_norm_a[0])
    gb = row(grp_norm_b[0])
    strips = _bias_strips(na_rel_bias[0])
    mb = jnp.where(np.arange(META_PAD)[None, :, None] < N_META,
                   jnp.pad(na_meta_bias[0].astype(f32) * LOG2_E, ((0, 0), (0, META_PAD - N_META)))[:, :, None], NEG)
    mb = jnp.broadcast_to(mb, (NA_HEADS, META_PAD, NA_BLOCK))

    xm = jnp.pad(meta_tokens.astype(f32), ((0, META_PAD - N_META), (0, 0)))
    mcos, msin = _rope_tables(jnp.pad(-jnp.ones((N_META,), f32), (0, META_PAD - N_META)),
                              jnp.pad(jnp.arange(N_META, dtype=f32), (0, META_PAD - N_META)))
    _, _, kam, vamt, _, kbm, vbmt = _ffn_in(xm, mcos, msin, p, tm=META_PAD, na_chunk=META_PAD,
                                            gqa_chunk=META_PAD)

    def run(x):
        batch, seq, _ = x.shape
        n_rows = seq // GRID_W
        tpos = jnp.arange(seq)
        cos, sin = _rope_tables((tpos // GRID_W).astype(f32), (tpos % GRID_W).astype(f32))
        h1, qa, ka, vat, qb, kb, vbt = _ffn_in(x.reshape(batch * seq, d), cos, sin, p, tm=TOKEN_TILE,
                                               na_chunk=NA_BLOCK,
                                               gqa_chunk=GQA_KEY_CHUNK if seq % GQA_KEY_CHUNK == 0 else seq)
        ya = _na_attn(qa, ka, vat, kam, vamt, strips, mb, ga, batch=batch, n_rows=n_rows)
        yb = _gqa_attn(qb, kb, vbt, kbm, vbmt, gb, batch=batch, seq=seq)
        return _out_ffn(h1, ya, yb, p, tm=TOKEN_TILE).reshape(batch, seq, d)

    return (run(x_prompt), run(x_sample))
```

```python
import functools

import numpy as np
import jax
import jax.numpy as jnp
from jax import lax
from jax.experimental import pallas as pl
from jax.experimental.pallas import tpu as pltpu

N_META = 16
GRID_W = 64
HEAD_DIM = 64
NA_HEADS = 8
NA_WIN_H = 8
NA_WIN_W = 16
GQA_HEADS = 8
GQA_KV_HEADS = 2
GQA_GROUP = GQA_HEADS // GQA_KV_HEADS
W_A = NA_HEADS * HEAD_DIM
W_B = GQA_HEADS * HEAD_DIM
W_KV = GQA_KV_HEADS * HEAD_DIM
ROPE_THETA = 10000.0
ROPE_AXIS_DIM = HEAD_DIM // 2
EPS = 1e-6

LANES = 128
NEG = -1e30
LOG2_E = 1.4426950408889634
META_PAD = 128
FF_CHUNK = 256
MIN_HALF_TILE = 256
TOKEN_TILE = 512
NA_ROWS = 4
NA_BLOCK = NA_ROWS * GRID_W
NA_KEY_ROWS = 3 * NA_ROWS
assert NA_ROWS == NA_WIN_H // 2
STRIP_D0 = NA_ROWS - 1
GQA_Q_TILE = 256
GQA_KEY_CHUNK = 8192
VMEM_LIMIT = 56 * 1024 * 1024

_NT = (((1,), (1,)), ((), ()))

f32 = jnp.float32
bf16 = jnp.bfloat16


def _rms(x, g):
    ms = jnp.mean(x * x, axis=-1, keepdims=True)
    return x * lax.rsqrt(ms + EPS) * g


def _const_spec(shape):
    nd = len(shape)
    return pl.BlockSpec(shape, lambda *_: (0,) * nd, pipeline_mode=pl.Buffered(1))


def _row_halves(tm):
    if tm < 2 * MIN_HALF_TILE:
        return (slice(0, tm),)
    return (slice(0, tm // 2), slice(tm // 2, tm))


def _swiglu_act(x, npre, wg_ref, wu_ref, act_ref):
    xn = _rms(x, npre).astype(bf16)
    d_ff = wg_ref.shape[1]
    for c0 in range(0, d_ff, FF_CHUNK):
        g = jnp.dot(xn, wg_ref[:, c0:c0 + FF_CHUNK], preferred_element_type=f32)
        u = jnp.dot(xn, wu_ref[:, c0:c0 + FF_CHUNK], preferred_element_type=f32)
        act_ref[:, c0:c0 + FF_CHUNK] = (g * jax.nn.sigmoid(g) * u).astype(bf16)


def _norm_rope_slab(x, g, cos, sin, lane):
    sq = x * x
    lo = jnp.sum(jnp.where(lane < HEAD_DIM, sq, 0.0), axis=-1, keepdims=True)
    hi = jnp.sum(jnp.where(lane < HEAD_DIM, 0.0, sq), axis=-1, keepdims=True)
    ms = jnp.where(lane < HEAD_DIM, lo, hi) * (1.0 / HEAD_DIM)
    xn = x * lax.rsqrt(ms + EPS) * g
    partner = jnp.where((lane % 32) < 16, pltpu.roll(xn, LANES - 16, 1), pltpu.roll(xn, 16, 1))
    return xn * cos + partner * sin


def _ffn_in_kernel(x_ref, cos_ref, sin_ref, n1pre_ref, wg_ref, wu_ref, wd_ref, n1post_ref,
                   mixpre_ref, win_ref, qn_ref, kn_ref,
                   h1_ref, qa_ref, ka_ref, vat_ref, qb_ref, kb_ref, vbt_ref, act_ref,
                   *, na_chunk):
    tm = x_ref.shape[0]
    halves = _row_halves(tm)
    _swiglu_act(x_ref[...], n1pre_ref[...], wg_ref, wu_ref, act_ref)
    ys = [jnp.dot(act_ref[r, :], wd_ref[...], preferred_element_type=f32) for r in halves]
    qscale = HEAD_DIM ** -0.5 * LOG2_E
    qn = qn_ref[...]
    kn = kn_ref[...]
    for r, y in zip(halves, ys):
        n = r.stop - r.start
        assert n % na_chunk == 0
        h1 = x_ref[r, :] + 0.5 * _rms(y, n1post_ref[...])
        h1_ref[r, :] = h1
        hn = _rms(h1, mixpre_ref[...]).astype(bf16)

        def proj(c0, width, hn=hn):
            return jnp.dot(hn, win_ref[:, c0:c0 + width], preferred_element_type=f32)

        lane = lax.broadcasted_iota(jnp.int32, (n, LANES), 1)
        cos = cos_ref[r, :]
        sin = sin_ref[r, :]
        kv = proj(3 * W_A + W_B, 2 * W_KV)
        kb_ref[r, :] = _norm_rope_slab(kv[:, :W_KV], kn, cos, sin, lane).astype(bf16)
        vbt_ref[0, :, r] = kv[:, W_KV:].T.astype(bf16)
        for s in range(0, W_B, 2 * LANES):
            qpair = proj(3 * W_A + s, 2 * LANES)
            for c0 in (0, LANES):
                qb_ref[r, s + c0:s + c0 + LANES] = (
                    _norm_rope_slab(qpair[:, c0:c0 + LANES], qn, cos, sin, lane) * qscale).astype(bf16)
        va = proj(2 * W_A, W_A)
        for j in range(n // na_chunk):
            vat_ref[r.start // na_chunk + j] = va[j * na_chunk:(j + 1) * na_chunk, :].T.astype(bf16)
        ka_ref[r, :] = proj(W_A, W_A).astype(bf16)
        qa_ref[r, :] = (proj(0, W_A) * qscale).astype(bf16)


def _ffn_in(x2, cos, sin, p, *, tm, na_chunk, gqa_chunk):
    t, d = x2.shape
    s_len = cos.shape[0]
    assert t % tm == 0 and s_len % tm == 0 and tm % na_chunk == 0
    assert gqa_chunk % tm == 0 and t % gqa_chunk == 0
    n_pos = s_len // tm
    tiles_per_chunk = gqa_chunk // tm
    d_ff = p['wg1'].shape[1]
    in_cols = p['w_in'].shape[1]
    tok = lambda w: pl.BlockSpec((tm, w), lambda i: (i, 0))
    pos = pl.BlockSpec((tm, LANES), lambda i: (i % n_pos, 0))
    out_shape = (
        jax.ShapeDtypeStruct((t, d), f32),
        jax.ShapeDtypeStruct((t, W_A), bf16),
        jax.ShapeDtypeStruct((t, W_A), bf16),
        jax.ShapeDtypeStruct((t // na_chunk, W_A, na_chunk), bf16),
        jax.ShapeDtypeStruct((t, W_B), bf16),
        jax.ShapeDtypeStruct((t, W_KV), bf16),
        jax.ShapeDtypeStruct((t // gqa_chunk, W_KV, gqa_chunk), bf16),
    )
    out_specs = (
        tok(d), tok(W_A), tok(W_A),
        pl.BlockSpec((tm // na_chunk, W_A, na_chunk), lambda i: (i, 0, 0)),
        tok(W_B), tok(W_KV),
        pl.BlockSpec((1, W_KV, tm), lambda i: (i // tiles_per_chunk, 0, i % tiles_per_chunk)),
    )
    in_specs = [
        tok(d), pos, pos,
        _const_spec((1, d)), _const_spec((d, d_ff)), _const_spec((d, d_ff)), _const_spec((d_ff, d)),
        _const_spec((1, d)), _const_spec((1, d)), _const_spec((d, in_cols)),
        _const_spec((1, LANES)), _const_spec((1, LANES)),
    ]
    return pl.pallas_call(
        functools.partial(_ffn_in_kernel, na_chunk=na_chunk),
        grid=(t // tm,),
        in_specs=in_specs,
        out_specs=out_specs,
        out_shape=out_shape,
        scratch_shapes=[pltpu.VMEM((tm, d_ff), bf16)],
        compiler_params=pltpu.CompilerParams(dimension_semantics=("arbitrary",),
                                             vmem_limit_bytes=VMEM_LIMIT),
        name="ffn_in",
    )(x2, cos, sin, p['n1pre'], p['wg1'], p['wu1'], p['wd1'], p['n1post'],
      p['mixpre'], p['w_in'], p['qn'], p['kn'])


def _na_kernel(q_ref, k0_ref, k1_ref, k2_ref, v0_ref, v1_ref, v2_ref, km_ref, vm_ref,
               strips_ref, mb_ref, g_ref, o_ref, kcat_ref, vcat_ref, s_ref, ot_ref):
    j = pl.program_id(1)
    n_blk = pl.num_programs(1)
    blk0 = jnp.clip(j - 1, 0, n_blk - 3)
    n_keys = NA_KEY_ROWS * GRID_W
    for p, (k_ref, v_ref) in enumerate(((k0_ref, v0_ref), (k1_ref, v1_ref), (k2_ref, v2_ref))):
        kcat_ref[p * NA_BLOCK:(p + 1) * NA_BLOCK, :] = k_ref[...]
        vcat_ref[:, p * NA_BLOCK:(p + 1) * NA_BLOCK] = v_ref[0]
    kcat_ref[n_keys:, :] = km_ref[...]
    vcat_ref[:, n_keys:] = vm_ref[0]

    n_strips = strips_ref.shape[2]
    strip0 = NA_ROWS * (blk0 - j) + NA_WIN_H - 1 - STRIP_D0 + 1
    strip_idx = [jnp.clip(strip0 + a, 0, n_strips - 1) for a in range(NA_KEY_ROWS)]
    lane = lax.broadcasted_iota(jnp.int32, (NA_BLOCK, LANES), 1)

    def scores(h):
        sl = slice(LANES * (h // 2), LANES * (h // 2) + LANES)
        in_head = (lane < HEAD_DIM) if h % 2 == 0 else (lane >= HEAD_DIM)
        qs = q_ref[:, sl]
        qm = jnp.where(in_head, qs, jnp.zeros_like(qs))
        s = lax.dot_general(kcat_ref[:, sl], qm, _NT, preferred_element_type=f32)
        s = s + jnp.concatenate([strips_ref[0, h, i] for i in strip_idx] + [mb_ref[h]], axis=0)
        s_ref[h % 2] = s
        return jnp.max(s, axis=0, keepdims=True)

    m = scores(0)
    for h in range(NA_HEADS):
        m_next = scores(h + 1) if h + 1 < NA_HEADS else None
        hs = slice(HEAD_DIM * h, HEAD_DIM * (h + 1))
        p = jnp.exp2(s_ref[h % 2] - m)
        l = jnp.sum(p, axis=0, keepdims=True)
        acc = jnp.dot(vcat_ref[hs, :], p.astype(bf16), preferred_element_type=f32)
        ot_ref[hs, :] = acc / l
        m = m_next
    o_ref[...] = _rms(ot_ref[...].T, g_ref[...]).astype(bf16)


def _na_attn(qa, ka, vat, km, vmt, strips, mb, g, *, batch, n_rows):
    t = qa.shape[0]
    assert n_rows % NA_ROWS == 0
    n_blk = n_rows // NA_ROWS
    assert n_blk >= 3 and t == batch * n_blk * NA_BLOCK
    n_cat = NA_KEY_ROWS * GRID_W + META_PAD

    def kv_idx(p):
        return lambda b, j: b * n_blk + jnp.clip(j - 1, 0, n_blk - 3) + p

    k_specs = [pl.BlockSpec((NA_BLOCK, W_A), (lambda f: lambda b, j: (f(b, j), 0))(kv_idx(p))) for p in range(3)]
    v_specs = [pl.BlockSpec((1, W_A, NA_BLOCK), (lambda f: lambda b, j: (f(b, j), 0, 0))(kv_idx(p)))
               for p in range(3)]
    q_spec = pl.BlockSpec((NA_BLOCK, W_A), lambda b, j: (b * n_blk + j, 0))
    strip_spec = pl.BlockSpec((1,) + strips.shape[1:],
                              lambda b, j: (jnp.where(j == 0, 0, jnp.where(j == n_blk - 1, 2, 1)), 0, 0, 0, 0))
    return pl.pallas_call(
        _na_kernel,
        grid=(batch, n_blk),
        in_specs=[q_spec, *k_specs, *v_specs,
                  _const_spec(km.shape), _const_spec(vmt.shape), strip_spec,
                  _const_spec(mb.shape), _const_spec(g.shape)],
        out_specs=pl.BlockSpec((NA_BLOCK, W_A), lambda b, j: (b * n_blk + j, 0)),
        out_shape=jax.ShapeDtypeStruct((t, W_A), bf16),
        scratch_shapes=[pltpu.VMEM((n_cat, W_A), bf16), pltpu.VMEM((W_A, n_cat), bf16),
                        pltpu.VMEM((2, n_cat, NA_BLOCK), f32), pltpu.VMEM((W_A, NA_BLOCK), f32)],
        compiler_params=pltpu.CompilerParams(dimension_semantics=("arbitrary", "arbitrary"),
                                             vmem_limit_bytes=VMEM_LIMIT),
        name="na_attn",
    )(qa, ka, ka, ka, vat, vat, vat, km, vmt, strips, mb, g)


def _loop(trips, body, init):
    return body(0, init) if trips == 1 else lax.fori_loop(0, trips, body, init)


def _chunk_rows(c, size):
    start = c * size
    return pl.ds(start if isinstance(c, int) else pl.multiple_of(start, size), size)


def _gqa_kernel(q_ref, q_next_ref, k_ref, vt_ref, km_ref, vmt_ref, g_ref, o_ref,
                s_ref, ot_ref, sm_ref, m_ref):
    assert GQA_HEADS % 2 == 0
    tq = q_ref.shape[0]
    n_chunks = vt_ref.shape[0]
    kc = vt_ref.shape[2]
    lane = lax.broadcasted_iota(jnp.int32, (tq, LANES), 1)
    meta_row = lax.broadcasted_iota(jnp.int32, (META_PAD, tq), 0)

    def head_query(q_ref, h):
        n = h // GQA_GROUP
        qs = q_ref[:, LANES * (h // 2):LANES * (h // 2) + LANES].astype(f32)
        if h % 2 != n:
            qs = pltpu.roll(qs, HEAD_DIM, 1)
        on_kv = (lane < HEAD_DIM) if n == 0 else (lane >= HEAD_DIM)
        return jnp.where(on_kv, qs, 0.0).astype(bf16)

    def meta_scores(qm):
        sm = lax.dot_general(km_ref[...], qm, _NT, preferred_element_type=f32)
        return jnp.where(meta_row < N_META, sm, NEG)

    def scores_chunk(h, qm, c, mx):
        rows = _chunk_rows(c, kc)
        s = lax.dot_general(k_ref[rows, :], qm, _NT, preferred_element_type=f32)
        s_ref[h % 2, rows, :] = s
        return jnp.maximum(mx, jnp.max(s, axis=0, keepdims=True))

    def weighted_chunk(h, m, c, l, acc):
        vs = slice(HEAD_DIM * (h // GQA_GROUP), HEAD_DIM * (h // GQA_GROUP + 1))
        p = jnp.exp2(s_ref[h % 2, _chunk_rows(c, kc), :] - m)
        l = l + jnp.sum(p, axis=0, keepdims=True)
        acc = acc + jnp.dot(vt_ref[c, vs, :], p.astype(bf16), preferred_element_type=f32)
        return l, acc

    @pl.when(pl.program_id(1) == 0)
    def _():
        qm = head_query(q_ref, 0)
        sm = meta_scores(qm)
        sm_ref[...] = sm
        m_ref[...] = _loop(n_chunks, functools.partial(scores_chunk, 0, qm), jnp.max(sm, axis=0, keepdims=True))

    sm = sm_ref[...]
    m = m_ref[...]
    for h in range(GQA_HEADS):
        vs = slice(HEAD_DIM * (h // GQA_GROUP), HEAD_DIM * (h // GQA_GROUP + 1))
        pm = jnp.exp2(sm - m)
        l = jnp.sum(pm, axis=0, keepdims=True)
        acc = jnp.dot(vmt_ref[0, vs, :], pm.astype(bf16), preferred_element_type=f32)
        qm_next = head_query(q_ref, h + 1) if h + 1 < GQA_HEADS else head_query(q_next_ref, 0)
        sm_next = meta_scores(qm_next)

        def both(c, carry, h=h, m=m, qm_next=qm_next):
            mx, l, acc = carry
            mx = scores_chunk(h + 1, qm_next, c, mx)
            l, acc = weighted_chunk(h, m, c, l, acc)
            return mx, l, acc

        m_next, l, acc = _loop(n_chunks, both, (jnp.max(sm_next, axis=0, keepdims=True), l, acc))
        ot_ref[HEAD_DIM * h:HEAD_DIM * (h + 1), :] = acc / l
        sm, m = sm_next, m_next
    sm_ref[...] = sm
    m_ref[...] = m
    o_ref[...] = _rms(ot_ref[...].T, g_ref[...]).astype(bf16)


def _gqa_attn(qb, kb, vbt, km, vmt, g, *, batch, seq):
    t = qb.shape[0]
    tq = GQA_Q_TILE
    kc = vbt.shape[2]
    assert t == batch * seq and seq % tq == 0 and seq % kc == 0
    n_q = seq // tq
    return pl.pallas_call(
        _gqa_kernel,
        grid=(batch, n_q),
        in_specs=[pl.BlockSpec((tq, W_B), lambda b, i: (b * n_q + i, 0)),
                  pl.BlockSpec((tq, W_B), lambda b, i: (b * n_q + jnp.minimum(i + 1, n_q - 1), 0)),
                  pl.BlockSpec((seq, W_KV), lambda b, i: (b, 0), pipeline_mode=pl.Buffered(1)),
                  pl.BlockSpec((seq // kc, W_KV, kc), lambda b, i: (b, 0, 0), pipeline_mode=pl.Buffered(1)),
                  _const_spec(km.shape), _const_spec(vmt.shape), _const_spec(g.shape)],
        out_specs=pl.BlockSpec((tq, W_B), lambda b, i: (b * n_q + i, 0)),
        out_shape=jax.ShapeDtypeStruct((t, W_B), bf16),
        scratch_shapes=[pltpu.VMEM((2, seq, tq), f32), pltpu.VMEM((W_B, tq), f32),
                        pltpu.VMEM((META_PAD, tq), f32), pltpu.VMEM((1, tq), f32)],
        compiler_params=pltpu.CompilerParams(dimension_semantics=("arbitrary", "arbitrary"),
                                             vmem_limit_bytes=VMEM_LIMIT),
        name="gqa_attn",
    )(qb, qb, kb, vbt, km, vmt, g)


def _out_ffn_kernel(h1_ref, ya_ref, yb_ref, woa_ref, wob_ref, mixpost_ref,
                    n2pre_ref, wg_ref, wu_ref, wd_ref, n2post_ref, fin_ref, o_ref, act_ref, h2_ref):
    halves = _row_halves(h1_ref.shape[0])
    ys = [jnp.dot(ya_ref[r, :], woa_ref[...], preferred_element_type=f32)
          + jnp.dot(yb_ref[r, :], wob_ref[...], preferred_element_type=f32) for r in halves]
    for r, y in zip(halves, ys):
        h2_ref[r, :] = h1_ref[r, :] + _rms(y, mixpost_ref[...])
    for r in halves:
        _swiglu_act(h2_ref[r, :], n2pre_ref[...], wg_ref, wu_ref, act_ref.at[r, :])
    ys = [jnp.dot(act_ref[r, :], wd_ref[...], preferred_element_type=f32) for r in halves]
    for r, y in zip(halves, ys):
        h3 = h2_ref[r, :] + 0.5 * _rms(y, n2post_ref[...])
        o_ref[r, :] = _rms(h3, fin_ref[...])


def _out_ffn(h1, ya, yb, p, *, tm):
    t, d = h1.shape
    d_ff = p['wg2'].shape[1]
    tok = lambda w: pl.BlockSpec((tm, w), lambda i: (i, 0))
    return pl.pallas_call(
        _out_ffn_kernel,
        grid=(t // tm,),
        in_specs=[tok(d), tok(W_A), tok(W_B),
                  _const_spec((W_A, d)), _const_spec((W_B, d)), _const_spec((1, d)),
                  _const_spec((1, d)), _const_spec((d, d_ff)), _const_spec((d, d_ff)),
                  _const_spec((d_ff, d)), _const_spec((1, d)), _const_spec((1, d))],
        out_specs=tok(d),
        out_shape=jax.ShapeDtypeStruct((t, d), f32),
        scratch_shapes=[pltpu.VMEM((tm, d_ff), bf16), pltpu.VMEM((tm, d), f32)],
        compiler_params=pltpu.CompilerParams(dimension_semantics=("arbitrary",),
                                             vmem_limit_bytes=VMEM_LIMIT),
        name="out_ffn",
    )(h1, ya, yb, p['woa'], p['wob'], p['mixpost'], p['n2pre'], p['wg2'], p['wu2'], p['wd2'],
      p['n2post'], p['fin'])


def _rope_tables(pos_row, pos_col):
    inv_freq = jnp.asarray(ROPE_THETA ** (-np.arange(0, ROPE_AXIS_DIM, 2) / ROPE_AXIS_DIM), f32)
    ang_r = pos_row[:, None] * inv_freq[None, :]
    ang_c = pos_col[:, None] * inv_freq[None, :]
    cos = jnp.concatenate([jnp.cos(ang_r)] * 2 + [jnp.cos(ang_c)] * 2, axis=-1)
    sin = jnp.concatenate([-jnp.sin(ang_r), jnp.sin(ang_r), -jnp.sin(ang_c), jnp.sin(ang_c)], axis=-1)
    return jnp.tile(cos, (1, 2)), jnp.tile(sin, (1, 2))


def _bias_strips(rel_bias):
    n_h, n_d, n_k = rel_bias.shape
    assert n_d == 2 * NA_WIN_H - 1 and n_k == 2 * NA_WIN_W - 1
    period = 2 * GRID_W - 1
    lead = GRID_W - NA_WIN_W
    z = jnp.concatenate([jnp.zeros((n_h, n_d, lead), f32), rel_bias.astype(f32)[..., ::-1],
                         jnp.zeros((n_h, n_d, period - lead - n_k), f32)], axis=-1)
    hank = jnp.tile(z, (1, 1, GRID_W + 1))[..., :GRID_W * (period + 1)]
    toep = hank.reshape(n_h, n_d, GRID_W, period + 1)[..., ::-1, :GRID_W]
    c = np.arange(GRID_W)[:, None]
    q = np.arange(GRID_W)[None, :]
    col_start = np.clip(q - NA_WIN_W // 2, 0, GRID_W - NA_WIN_W)
    toep = jnp.where((c >= col_start) & (c < col_start + NA_WIN_W), toep * LOG2_E, NEG)
    n_strips = n_d - STRIP_D0
    strips = jnp.concatenate([toep[:, STRIP_D0 - b:STRIP_D0 - b + n_strips] for b in range(NA_ROWS)], axis=-1)
    b = np.arange(NA_ROWS)[None, :]
    rel = np.arange(n_strips)[:, None] + STRIP_D0 - b - (NA_WIN_H - 1)
    half = NA_WIN_H // 2
    row_ok = np.stack([(rel >= -b) & (rel < NA_WIN_H - b),
                       (rel >= -half) & (rel < half),
                       (rel >= -half - b) & (rel < half - b)])
    row_ok = np.repeat(row_ok, GRID_W, axis=-1)[:, None, :, None, :]
    strips = jnp.where(row_ok, strips[None], NEG)
    return jnp.pad(strips, ((0, 0), (0, 0), (1, 1), (0, 0), (0, 0)), constant_values=NEG)


def kernel(x_prompt, x_sample, meta_tokens, ffn1_norm_pre, ffn1_w_gate, ffn1_w_up, ffn1_w_down, ffn1_norm_post, mix_norm_pre, w_in, na_rel_bias, na_meta_bias, gqa_q_norm, gqa_k_norm, grp_norm_a, grp_norm_b, w_out, mix_norm_post, ffn2_norm_pre, ffn2_w_gate, ffn2_w_up, ffn2_w_down, ffn2_norm_post, final_norm):
    assert ffn1_w_gate.shape[0] == 1, "meta keys/values are shared across the batch only for depth 1"
    d = x_prompt.shape[-1]
    row = lambda v: v.reshape(1, -1).astype(f32)
    p = dict(
        n1pre=row(ffn1_norm_pre[0]), wg1=ffn1_w_gate[0].astype(bf16), wu1=ffn1_w_up[0].astype(bf16),
        wd1=ffn1_w_down[0].astype(bf16), n1post=row(ffn1_norm_post[0]),
        mixpre=row(mix_norm_pre[0]), w_in=w_in[0].astype(bf16),
        qn=row(jnp.tile(gqa_q_norm[0], 2)), kn=row(jnp.tile(gqa_k_norm[0], 2)),
        woa=w_out[0, :W_A].astype(bf16), wob=w_out[0, W_A:].astype(bf16), mixpost=row(mix_norm_post[0]),
        n2pre=row(ffn2_norm_pre[0]), wg2=ffn2_w_gate[0].astype(bf16), wu2=ffn2_w_up[0].astype(bf16),
        wd2=ffn2_w_down[0].astype(bf16), n2post=row(ffn2_norm_post[0]), fin=row(final_norm),
    )
    ga = row(grp_norm_a[0])
    gb = row(grp_norm_b[0])
    strips = _bias_strips(na_rel_bias[0])
    mb = jnp.where(np.arange(META_PAD)[None, :, None] < N_META,
                   jnp.pad(na_meta_bias[0].astype(f32) * LOG2_E, ((0, 0), (0, META_PAD - N_META)))[:, :, None], NEG)
    mb = jnp.broadcast_to(mb, (NA_HEADS, META_PAD, NA_BLOCK))

    xm = jnp.pad(meta_tokens.astype(f32), ((0, META_PAD - N_META), (0, 0)))
    mcos, msin = _rope_tables(jnp.pad(-jnp.ones((N_META,), f32), (0, META_PAD - N_META)),
                              jnp.pad(jnp.arange(N_META, dtype=f32), (0, META_PAD - N_META)))
    _, _, kam, vamt, _, kbm, vbmt = _ffn_in(xm, mcos, msin, p, tm=META_PAD, na_chunk=META_PAD,
                                            gqa_chunk=META_PAD)

    def run(x):
        batch, seq, _ = x.shape
        n_rows = seq // GRID_W
        tpos = jnp.arange(seq)
        cos, sin = _rope_tables((tpos // GRID_W).astype(f32), (tpos % GRID_W).astype(f32))
        h1, qa, ka, vat, qb, kb, vbt = _ffn_in(x.reshape(batch * seq, d), cos, sin, p, tm=TOKEN_TILE,
                                               na_chunk=NA_BLOCK,
                                               gqa_chunk=GQA_KEY_CHUNK if seq % GQA_KEY_CHUNK == 0 else seq)
        ya = _na_attn(qa, ka, vat, kam, vamt, strips, mb, ga, batch=batch, n_rows=n_rows)
        yb = _gqa_attn(qb, kb, vbt, kbm, vbmt, gb, batch=batch, seq=seq)
        return _out_ffn(h1, ya, yb, p, tm=TOKEN_TILE).reshape(batch, seq, d)

    return (run(x_prompt), run(x_sample))
```

```python
import functools

import numpy as np
import jax
import jax.numpy as jnp
from jax import lax
from jax.experimental import pallas as pl
from jax.experimental.pallas import tpu as pltpu

N_META = 16
GRID_W = 64
HEAD_DIM = 64
NA_HEADS = 8
NA_WIN_H = 8
NA_WIN_W = 16
GQA_HEADS = 8
GQA_KV_HEADS = 2
GQA_GROUP = GQA_HEADS // GQA_KV_HEADS
W_A = NA_HEADS * HEAD_DIM
W_B = GQA_HEADS * HEAD_DIM
W_KV = GQA_KV_HEADS * HEAD_DIM
ROPE_THETA = 10000.0
ROPE_AXIS_DIM = HEAD_DIM // 2
EPS = 1e-6

LANES = 128
NEG = -1e30
LOG2_E = 1.4426950408889634
META_PAD = 128
FF_CHUNK = 256
MIN_HALF_TILE = 256
TOKEN_TILE = 512
NA_ROWS = 4
NA_BLOCK = NA_ROWS * GRID_W
NA_KEY_ROWS = 3 * NA_ROWS
assert NA_ROWS == NA_WIN_H // 2
STRIP_D0 = NA_ROWS - 1
GQA_Q_TILE = 256
GQA_KEY_CHUNK = 16384
VMEM_LIMIT = 56 * 1024 * 1024

_NT = (((1,), (1,)), ((), ()))

f32 = jnp.float32
bf16 = jnp.bfloat16


def _rms(x, g):
    ms = jnp.mean(x * x, axis=-1, keepdims=True)
    return x * lax.rsqrt(ms + EPS) * g


def _const_spec(shape):
    nd = len(shape)
    return pl.BlockSpec(shape, lambda *_: (0,) * nd, pipeline_mode=pl.Buffered(1))


def _row_halves(tm):
    if tm < 2 * MIN_HALF_TILE:
        return (slice(0, tm),)
    return (slice(0, tm // 2), slice(tm // 2, tm))


def _swiglu_act(x, npre, wg_ref, wu_ref, act_ref):
    xn = _rms(x, npre).astype(bf16)
    d_ff = wg_ref.shape[1]
    for c0 in range(0, d_ff, FF_CHUNK):
        g = jnp.dot(xn, wg_ref[:, c0:c0 + FF_CHUNK], preferred_element_type=f32)
        u = jnp.dot(xn, wu_ref[:, c0:c0 + FF_CHUNK], preferred_element_type=f32)
        act_ref[:, c0:c0 + FF_CHUNK] = (g * jax.nn.sigmoid(g) * u).astype(bf16)


def _norm_rope_slab(x, g, cos, sin, lane):
    sq = x * x
    lo = jnp.sum(jnp.where(lane < HEAD_DIM, sq, 0.0), axis=-1, keepdims=True)
    hi = jnp.sum(jnp.where(lane < HEAD_DIM, 0.0, sq), axis=-1, keepdims=True)
    ms = jnp.where(lane < HEAD_DIM, lo, hi) * (1.0 / HEAD_DIM)
    xn = x * lax.rsqrt(ms + EPS) * g
    partner = jnp.where((lane % 32) < 16, pltpu.roll(xn, LANES - 16, 1), pltpu.roll(xn, 16, 1))
    return xn * cos + partner * sin


def _ffn_in_kernel(x_ref, cos_ref, sin_ref, n1pre_ref, wg_ref, wu_ref, wd_ref, n1post_ref,
                   mixpre_ref, win_ref, qn_ref, kn_ref,
                   h1_ref, qa_ref, ka_ref, vat_ref, qb_ref, kb_ref, vbt_ref, act_ref,
                   *, na_chunk):
    tm = x_ref.shape[0]
    halves = _row_halves(tm)
    _swiglu_act(x_ref[...], n1pre_ref[...], wg_ref, wu_ref, act_ref)
    ys = [jnp.dot(act_ref[r, :], wd_ref[...], preferred_element_type=f32) for r in halves]
    qscale = HEAD_DIM ** -0.5 * LOG2_E
    qn = qn_ref[...]
    kn = kn_ref[...]
    for r, y in zip(halves, ys):
        n = r.stop - r.start
        assert n % na_chunk == 0
        h1 = x_ref[r, :] + 0.5 * _rms(y, n1post_ref[...])
        h1_ref[r, :] = h1
        hn = _rms(h1, mixpre_ref[...]).astype(bf16)

        def proj(c0, width, hn=hn):
            return jnp.dot(hn, win_ref[:, c0:c0 + width], preferred_element_type=f32)

        lane = lax.broadcasted_iota(jnp.int32, (n, LANES), 1)
        cos = cos_ref[r, :]
        sin = sin_ref[r, :]
        kv = proj(3 * W_A + W_B, 2 * W_KV)
        kb_ref[r, :] = _norm_rope_slab(kv[:, :W_KV], kn, cos, sin, lane).astype(bf16)
        vbt_ref[0, :, r] = kv[:, W_KV:].T.astype(bf16)
        for s in range(0, W_B, 2 * LANES):
            qpair = proj(3 * W_A + s, 2 * LANES)
            for c0 in (0, LANES):
                qb_ref[r, s + c0:s + c0 + LANES] = (
                    _norm_rope_slab(qpair[:, c0:c0 + LANES], qn, cos, sin, lane) * qscale).astype(bf16)
        va = proj(2 * W_A, W_A)
        for j in range(n // na_chunk):
            vat_ref[r.start // na_chunk + j] = va[j * na_chunk:(j + 1) * na_chunk, :].T.astype(bf16)
        ka_ref[r, :] = proj(W_A, W_A).astype(bf16)
        qa_ref[r, :] = (proj(0, W_A) * qscale).astype(bf16)


def _ffn_in(x2, cos, sin, p, *, tm, na_chunk, gqa_chunk):
    t, d = x2.shape
    s_len = cos.shape[0]
    assert t % tm == 0 and s_len % tm == 0 and tm % na_chunk == 0
    assert gqa_chunk % tm == 0 and t % gqa_chunk == 0
    n_pos = s_len // tm
    tiles_per_chunk = gqa_chunk // tm
    d_ff = p['wg1'].shape[1]
    in_cols = p['w_in'].shape[1]
    tok = lambda w: pl.BlockSpec((tm, w), lambda i: (i, 0))
    pos = pl.BlockSpec((tm, LANES), lambda i: (i % n_pos, 0))
    out_shape = (
        jax.ShapeDtypeStruct((t, d), f32),
        jax.ShapeDtypeStruct((t, W_A), bf16),
        jax.ShapeDtypeStruct((t, W_A), bf16),
        jax.ShapeDtypeStruct((t // na_chunk, W_A, na_chunk), bf16),
        jax.ShapeDtypeStruct((t, W_B), bf16),
        jax.ShapeDtypeStruct((t, W_KV), bf16),
        jax.ShapeDtypeStruct((t // gqa_chunk, W_KV, gqa_chunk), bf16),
    )
    out_specs = (
        tok(d), tok(W_A), tok(W_A),
        pl.BlockSpec((tm // na_chunk, W_A, na_chunk), lambda i: (i, 0, 0)),
        tok(W_B), tok(W_KV),
        pl.BlockSpec((1, W_KV, tm), lambda i: (i // tiles_per_chunk, 0, i % tiles_per_chunk)),
    )
    in_specs = [
        tok(d), pos, pos,
        _const_spec((1, d)), _const_spec((d, d_ff)), _const_spec((d, d_ff)), _const_spec((d_ff, d)),
        _const_spec((1, d)), _const_spec((1, d)), _const_spec((d, in_cols)),
        _const_spec((1, LANES)), _const_spec((1, LANES)),
    ]
    return pl.pallas_call(
        functools.partial(_ffn_in_kernel, na_chunk=na_chunk),
        grid=(t // tm,),
        in_specs=in_specs,
        out_specs=out_specs,
        out_shape=out_shape,
        scratch_shapes=[pltpu.VMEM((tm, d_ff), bf16)],
        compiler_params=pltpu.CompilerParams(dimension_semantics=("arbitrary",),
                                             vmem_limit_bytes=VMEM_LIMIT),
        name="ffn_in",
    )(x2, cos, sin, p['n1pre'], p['wg1'], p['wu1'], p['wd1'], p['n1post'],
      p['mixpre'], p['w_in'], p['qn'], p['kn'])


def _na_kernel(q_ref, k0_ref, k1_ref, k2_ref, v0_ref, v1_ref, v2_ref, km_ref, vm_ref,
               strips_ref, mb_ref, g_ref, o_ref, kcat_ref, vcat_ref, s_ref, ot_ref):
    j = pl.program_id(1)
    n_blk = pl.num_programs(1)
    blk0 = jnp.clip(j - 1, 0, n_blk - 3)
    n_keys = NA_KEY_ROWS * GRID_W
    for p, (k_ref, v_ref) in enumerate(((k0_ref, v0_ref), (k1_ref, v1_ref), (k2_ref, v2_ref))):
        kcat_ref[p * NA_BLOCK:(p + 1) * NA_BLOCK, :] = k_ref[...]
        vcat_ref[:, p * NA_BLOCK:(p + 1) * NA_BLOCK] = v_ref[0]
    kcat_ref[n_keys:, :] = km_ref[...]
    vcat_ref[:, n_keys:] = vm_ref[0]

    n_strips = strips_ref.shape[2]
    strip0 = NA_ROWS * (blk0 - j) + NA_WIN_H - 1 - STRIP_D0 + 1
    strip_idx = [jnp.clip(strip0 + a, 0, n_strips - 1) for a in range(NA_KEY_ROWS)]
    lane = lax.broadcasted_iota(jnp.int32, (NA_BLOCK, LANES), 1)

    def scores(h):
        sl = slice(LANES * (h // 2), LANES * (h // 2) + LANES)
        in_head = (lane < HEAD_DIM) if h % 2 == 0 else (lane >= HEAD_DIM)
        qs = q_ref[:, sl]
        qm = jnp.where(in_head, qs, jnp.zeros_like(qs))
        s = lax.dot_general(kcat_ref[:, sl], qm, _NT, preferred_element_type=f32)
        s = s + jnp.concatenate([strips_ref[0, h, i] for i in strip_idx] + [mb_ref[h]], axis=0)
        s_ref[h % 2] = s
        return jnp.max(s, axis=0, keepdims=True)

    m = scores(0)
    for h in range(NA_HEADS):
        m_next = scores(h + 1) if h + 1 < NA_HEADS else None
        hs = slice(HEAD_DIM * h, HEAD_DIM * (h + 1))
        p = jnp.exp2(s_ref[h % 2] - m)
        l = jnp.sum(p, axis=0, keepdims=True)
        acc = jnp.dot(vcat_ref[hs, :], p.astype(bf16), preferred_element_type=f32)
        ot_ref[hs, :] = acc / l
        m = m_next
    o_ref[...] = _rms(ot_ref[...].T, g_ref[...]).astype(bf16)


def _na_attn(qa, ka, vat, km, vmt, strips, mb, g, *, batch, n_rows):
    t = qa.shape[0]
    assert n_rows % NA_ROWS == 0
    n_blk = n_rows // NA_ROWS
    assert n_blk >= 3 and t == batch * n_blk * NA_BLOCK
    n_cat = NA_KEY_ROWS * GRID_W + META_PAD

    def kv_idx(p):
        return lambda b, j: b * n_blk + jnp.clip(j - 1, 0, n_blk - 3) + p

    k_specs = [pl.BlockSpec((NA_BLOCK, W_A), (lambda f: lambda b, j: (f(b, j), 0))(kv_idx(p))) for p in range(3)]
    v_specs = [pl.BlockSpec((1, W_A, NA_BLOCK), (lambda f: lambda b, j: (f(b, j), 0, 0))(kv_idx(p)))
               for p in range(3)]
    q_spec = pl.BlockSpec((NA_BLOCK, W_A), lambda b, j: (b * n_blk + j, 0))
    strip_spec = pl.BlockSpec((1,) + strips.shape[1:],
                              lambda b, j: (jnp.where(j == 0, 0, jnp.where(j == n_blk - 1, 2, 1)), 0, 0, 0, 0))
    return pl.pallas_call(
        _na_kernel,
        grid=(batch, n_blk),
        in_specs=[q_spec, *k_specs, *v_specs,
                  _const_spec(km.shape), _const_spec(vmt.shape), strip_spec,
                  _const_spec(mb.shape), _const_spec(g.shape)],
        out_specs=pl.BlockSpec((NA_BLOCK, W_A), lambda b, j: (b * n_blk + j, 0)),
        out_shape=jax.ShapeDtypeStruct((t, W_A), bf16),
        scratch_shapes=[pltpu.VMEM((n_cat, W_A), bf16), pltpu.VMEM((W_A, n_cat), bf16),
                        pltpu.VMEM((2, n_cat, NA_BLOCK), f32), pltpu.VMEM((W_A, NA_BLOCK), f32)],
        compiler_params=pltpu.CompilerParams(dimension_semantics=("arbitrary", "arbitrary"),
                                             vmem_limit_bytes=VMEM_LIMIT),
        name="na_attn",
    )(qa, ka, ka, ka, vat, vat, vat, km, vmt, strips, mb, g)


def _loop(trips, body, init):
    return body(0, init) if trips == 1 else lax.fori_loop(0, trips, body, init)


def _chunk_rows(c, size):
    start = c * size
    return pl.ds(start if isinstance(c, int) else pl.multiple_of(start, size), size)


def _gqa_kernel(q_ref, q_next_ref, k_ref, vt_ref, km_ref, vmt_ref, g_ref, o_ref,
                s_ref, ot_ref, sm_ref, m_ref):
    assert GQA_HEADS % 2 == 0
    tq = q_ref.shape[0]
    n_chunks = vt_ref.shape[0]
    kc = vt_ref.shape[2]
    lane = lax.broadcasted_iota(jnp.int32, (tq, LANES), 1)
    meta_row = lax.broadcasted_iota(jnp.int32, (META_PAD, tq), 0)

    def head_query(q_ref, h):
        n = h // GQA_GROUP
        qs = q_ref[:, LANES * (h // 2):LANES * (h // 2) + LANES].astype(f32)
        if h % 2 != n:
            qs = pltpu.roll(qs, HEAD_DIM, 1)
        on_kv = (lane < HEAD_DIM) if n == 0 else (lane >= HEAD_DIM)
        return jnp.where(on_kv, qs, 0.0).astype(bf16)

    def meta_scores(qm):
        sm = lax.dot_general(km_ref[...], qm, _NT, preferred_element_type=f32)
        return jnp.where(meta_row < N_META, sm, NEG)

    def scores_chunk(h, qm, c, mx):
        rows = _chunk_rows(c, kc)
        s = lax.dot_general(k_ref[rows, :], qm, _NT, preferred_element_type=f32)
        s_ref[h % 2, rows, :] = s
        return jnp.maximum(mx, jnp.max(s, axis=0, keepdims=True))

    def weighted_chunk(h, m, c, l, acc):
        vs = slice(HEAD_DIM * (h // GQA_GROUP), HEAD_DIM * (h // GQA_GROUP + 1))
        p = jnp.exp2(s_ref[h % 2, _chunk_rows(c, kc), :] - m)
        l = l + jnp.sum(p, axis=0, keepdims=True)
        acc = acc + jnp.dot(vt_ref[c, vs, :], p.astype(bf16), preferred_element_type=f32)
        return l, acc

    @pl.when(pl.program_id(1) == 0)
    def _():
        qm = head_query(q_ref, 0)
        sm = meta_scores(qm)
        sm_ref[...] = sm
        m_ref[...] = _loop(n_chunks, functools.partial(scores_chunk, 0, qm), jnp.max(sm, axis=0, keepdims=True))

    sm = sm_ref[...]
    m = m_ref[...]
    for h in range(GQA_HEADS):
        vs = slice(HEAD_DIM * (h // GQA_GROUP), HEAD_DIM * (h // GQA_GROUP + 1))
        pm = jnp.exp2(sm - m)
        l = jnp.sum(pm, axis=0, keepdims=True)
        acc = jnp.dot(vmt_ref[0, vs, :], pm.astype(bf16), preferred_element_type=f32)
        qm_next = head_query(q_ref, h + 1) if h + 1 < GQA_HEADS else head_query(q_next_ref, 0)
        sm_next = meta_scores(qm_next)

        def both(c, carry, h=h, m=m, qm_next=qm_next):
            mx, l, acc = carry
            mx = scores_chunk(h + 1, qm_next, c, mx)
            l, acc = weighted_chunk(h, m, c, l, acc)
            return mx, l, acc

        m_next, l, acc = _loop(n_chunks, both, (jnp.max(sm_next, axis=0, keepdims=True), l, acc))
        ot_ref[HEAD_DIM * h:HEAD_DIM * (h + 1), :] = acc / l
        sm, m = sm_next, m_next
    sm_ref[...] = sm
    m_ref[...] = m
    o_ref[...] = _rms(ot_ref[...].T, g_ref[...]).astype(bf16)


def _gqa_attn(qb, kb, vbt, km, vmt, g, *, batch, seq):
    t = qb.shape[0]
    tq = GQA_Q_TILE
    kc = vbt.shape[2]
    assert t == batch * seq and seq % tq == 0 and seq % kc == 0
    n_q = seq // tq
    return pl.pallas_call(
        _gqa_kernel,
        grid=(batch, n_q),
        in_specs=[pl.BlockSpec((tq, W_B), lambda b, i: (b * n_q + i, 0)),
                  pl.BlockSpec((tq, W_B), lambda b, i: (b * n_q + jnp.minimum(i + 1, n_q - 1), 0)),
                  pl.BlockSpec((seq, W_KV), lambda b, i: (b, 0), pipeline_mode=pl.Buffered(1)),
                  pl.BlockSpec((seq // kc, W_KV, kc), lambda b, i: (b, 0, 0), pipeline_mode=pl.Buffered(1)),
                  _const_spec(km.shape), _const_spec(vmt.shape), _const_spec(g.shape)],
        out_specs=pl.BlockSpec((tq, W_B), lambda b, i: (b * n_q + i, 0)),
        out_shape=jax.ShapeDtypeStruct((t, W_B), bf16),
        scratch_shapes=[pltpu.VMEM((2, seq, tq), f32), pltpu.VMEM((W_B, tq), f32),
                        pltpu.VMEM((META_PAD, tq), f32), pltpu.VMEM((1, tq), f32)],
        compiler_params=pltpu.CompilerParams(dimension_semantics=("arbitrary", "arbitrary"),
                                             vmem_limit_bytes=VMEM_LIMIT),
        name="gqa_attn",
    )(qb, qb, kb, vbt, km, vmt, g)


def _out_ffn_kernel(h1_ref, ya_ref, yb_ref, woa_ref, wob_ref, mixpost_ref,
                    n2pre_ref, wg_ref, wu_ref, wd_ref, n2post_ref, fin_ref, o_ref, act_ref, h2_ref):
    halves = _row_halves(h1_ref.shape[0])
    ys = [jnp.dot(ya_ref[r, :], woa_ref[...], preferred_element_type=f32)
          + jnp.dot(yb_ref[r, :], wob_ref[...], preferred_element_type=f32) for r in halves]
    for r, y in zip(halves, ys):
        h2_ref[r, :] = h1_ref[r, :] + _rms(y, mixpost_ref[...])
    for r in halves:
        _swiglu_act(h2_ref[r, :], n2pre_ref[...], wg_ref, wu_ref, act_ref.at[r, :])
    ys = [jnp.dot(act_ref[r, :], wd_ref[...], preferred_element_type=f32) for r in halves]
    for r, y in zip(halves, ys):
        h3 = h2_ref[r, :] + 0.5 * _rms(y, n2post_ref[...])
        o_ref[r, :] = _rms(h3, fin_ref[...])


def _out_ffn(h1, ya, yb, p, *, tm):
    t, d = h1.shape
    d_ff = p['wg2'].shape[1]
    tok = lambda w: pl.BlockSpec((tm, w), lambda i: (i, 0))
    return pl.pallas_call(
        _out_ffn_kernel,
        grid=(t // tm,),
        in_specs=[tok(d), tok(W_A), tok(W_B),
                  _const_spec((W_A, d)), _const_spec((W_B, d)), _const_spec((1, d)),
                  _const_spec((1, d)), _const_spec((d, d_ff)), _const_spec((d, d_ff)),
                  _const_spec((d_ff, d)), _const_spec((1, d)), _const_spec((1, d))],
        out_specs=tok(d),
        out_shape=jax.ShapeDtypeStruct((t, d), f32),
        scratch_shapes=[pltpu.VMEM((tm, d_ff), bf16), pltpu.VMEM((tm, d), f32)],
        compiler_params=pltpu.CompilerParams(dimension_semantics=("arbitrary",),
                                             vmem_limit_bytes=VMEM_LIMIT),
        name="out_ffn",
    )(h1, ya, yb, p['woa'], p['wob'], p['mixpost'], p['n2pre'], p['wg2'], p['wu2'], p['wd2'],
      p['n2post'], p['fin'])


def _rope_tables(pos_row, pos_col):
    inv_freq = jnp.asarray(ROPE_THETA ** (-np.arange(0, ROPE_AXIS_DIM, 2) / ROPE_AXIS_DIM), f32)
    ang_r = pos_row[:, None] * inv_freq[None, :]
    ang_c = pos_col[:, None] * inv_freq[None, :]
    cos = jnp.concatenate([jnp.cos(ang_r)] * 2 + [jnp.cos(ang_c)] * 2, axis=-1)
    sin = jnp.concatenate([-jnp.sin(ang_r), jnp.sin(ang_r), -jnp.sin(ang_c), jnp.sin(ang_c)], axis=-1)
    return jnp.tile(cos, (1, 2)), jnp.tile(sin, (1, 2))


def _bias_strips(rel_bias):
    n_h, n_d, n_k = rel_bias.shape
    assert n_d == 2 * NA_WIN_H - 1 and n_k == 2 * NA_WIN_W - 1
    period = 2 * GRID_W - 1
    lead = GRID_W - NA_WIN_W
    z = jnp.concatenate([jnp.zeros((n_h, n_d, lead), f32), rel_bias.astype(f32)[..., ::-1],
                         jnp.zeros((n_h, n_d, period - lead - n_k), f32)], axis=-1)
    hank = jnp.tile(z, (1, 1, GRID_W + 1))[..., :GRID_W * (period + 1)]
    toep = hank.reshape(n_h, n_d, GRID_W, period + 1)[..., ::-1, :GRID_W]
    c = np.arange(GRID_W)[:, None]
    q = np.arange(GRID_W)[None, :]
    col_start = np.clip(q - NA_WIN_W // 2, 0, GRID_W - NA_WIN_W)
    toep = jnp.where((c >= col_start) & (c < col_start + NA_WIN_W), toep * LOG2_E, NEG)
    n_strips = n_d - STRIP_D0
    strips = jnp.concatenate([toep[:, STRIP_D0 - b:STRIP_D0 - b + n_strips] for b in range(NA_ROWS)], axis=-1)
    b = np.arange(NA_ROWS)[None, :]
    rel = np.arange(n_strips)[:, None] + STRIP_D0 - b - (NA_WIN_H - 1)
    half = NA_WIN_H // 2
    row_ok = np.stack([(rel >= -b) & (rel < NA_WIN_H - b),
                       (rel >= -half) & (rel < half),
                       (rel >= -half - b) & (rel < half - b)])
    row_ok = np.repeat(row_ok, GRID_W, axis=-1)[:, None, :, None, :]
    strips = jnp.where(row_ok, strips[None], NEG)
    return jnp.pad(strips, ((0, 0), (0, 0), (1, 1), (0, 0), (0, 0)), constant_values=NEG)


def kernel(x_prompt, x_sample, meta_tokens, ffn1_norm_pre, ffn1_w_gate, ffn1_w_up, ffn1_w_down, ffn1_norm_post, mix_norm_pre, w_in, na_rel_bias, na_meta_bias, gqa_q_norm, gqa_k_norm, grp_norm_a, grp_norm_b, w_out, mix_norm_post, ffn2_norm_pre, ffn2_w_gate, ffn2_w_up, ffn2_w_down, ffn2_norm_post, final_norm):
    assert ffn1_w_gate.shape[0] == 1, "meta keys/values are shared across the batch only for depth 1"
    d = x_prompt.shape[-1]
    row = lambda v: v.reshape(1, -1).astype(f32)
    p = dict(
        n1pre=row(ffn1_norm_pre[0]), wg1=ffn1_w_gate[0].astype(bf16), wu1=ffn1_w_up[0].astype(bf16),
        wd1=ffn1_w_down[0].astype(bf16), n1post=row(ffn1_norm_post[0]),
        mixpre=row(mix_norm_pre[0]), w_in=w_in[0].astype(bf16),
        qn=row(jnp.tile(gqa_q_norm[0], 2)), kn=row(jnp.tile(gqa_k_norm[0], 2)),
        woa=w_out[0, :W_A].astype(bf16), wob=w_out[0, W_A:].astype(bf16), mixpost=row(mix_norm_post[0]),
        n2pre=row(ffn2_norm_pre[0]), wg2=ffn2_w_gate[0].astype(bf16), wu2=ffn2_w_up[0].astype(bf16),
        wd2=ffn2_w_down[0].astype(bf16), n2post=row(ffn2_norm_post[0]), fin=row(final_norm),
    )
    ga = row(grp_norm_a[0])
    gb = row(grp_norm_b[0])
    strips = _bias_strips(na_rel_bias[0])
    mb = jnp.where(np.arange(META_PAD)[None, :, None] < N_META,
                   jnp.pad(na_meta_bias[0].astype(f32) * LOG2_E, ((0, 0), (0, META_PAD - N_META)))[:, :, None], NEG)
    mb = jnp.broadcast_to(mb, (NA_HEADS, META_PAD, NA_BLOCK))

    xm = jnp.pad(meta_tokens.astype(f32), ((0, META_PAD - N_META), (0, 0)))
    mcos, msin = _rope_tables(jnp.pad(-jnp.ones((N_META,), f32), (0, META_PAD - N_META)),
                              jnp.pad(jnp.arange(N_META, dtype=f32), (0, META_PAD - N_META)))
    _, _, kam, vamt, _, kbm, vbmt = _ffn_in(xm, mcos, msin, p, tm=META_PAD, na_chunk=META_PAD,
                                            gqa_chunk=META_PAD)

    def run(x):
        batch, seq, _ = x.shape
        n_rows = seq // GRID_W
        tpos = jnp.arange(seq)
        cos, sin = _rope_tables((tpos // GRID_W).astype(f32), (tpos % GRID_W).astype(f32))
        h1, qa, ka, vat, qb, kb, vbt = _ffn_in(x.reshape(batch * seq, d), cos, sin, p, tm=TOKEN_TILE,
                                               na_chunk=NA_BLOCK,
                                               gqa_chunk=GQA_KEY_CHUNK if seq % GQA_KEY_CHUNK == 0 else seq)
        ya = _na_attn(qa, ka, vat, kam, vamt, strips, mb, ga, batch=batch, n_rows=n_rows)
        yb = _gqa_attn(qb, kb, vbt, kbm, vbmt, gb, batch=batch, seq=seq)
        return _out_ffn(h1, ya, yb, p, tm=TOKEN_TILE).reshape(batch, seq, d)

    return (run(x_prompt), run(x_sample))
```

```python
import functools

import numpy as np
import jax
import jax.numpy as jnp
from jax import lax
from jax.experimental import pallas as pl
from jax.experimental.pallas import tpu as pltpu

N_META = 16
GRID_W = 64
HEAD_DIM = 64
NA_HEADS = 8
NA_WIN_H = 8
NA_WIN_W = 16
GQA_HEADS = 8
GQA_KV_HEADS = 2
GQA_GROUP = GQA_HEADS // GQA_KV_HEADS
W_A = NA_HEADS * HEAD_DIM
W_B = GQA_HEADS * HEAD_DIM
W_KV = GQA_KV_HEADS * HEAD_DIM
ROPE_THETA = 10000.0
ROPE_AXIS_DIM = HEAD_DIM // 2
EPS = 1e-6

LANES = 128
NEG = -1e30
LOG2_E = 1.4426950408889634
META_PAD = 128
FF_CHUNK = 256
MIN_HALF_TILE = 256
TOKEN_TILE = 512
NA_ROWS = 4
NA_BLOCK = NA_ROWS * GRID_W
NA_LOOKAHEAD = 2
NA_KEY_ROWS = 3 * NA_ROWS
assert NA_ROWS == NA_WIN_H // 2
STRIP_D0 = NA_ROWS - 1
GQA_Q_TILE = 256
GQA_SCORE_BUFFERS = (2, 4)
GQA_SCORE_BYTES = 32 * 1024 * 1024
VMEM_LIMIT = 56 * 1024 * 1024

_NT = (((1,), (1,)), ((), ()))

f32 = jnp.float32
bf16 = jnp.bfloat16


def _rms(x, g):
    ms = jnp.mean(x * x, axis=-1, keepdims=True)
    return x * lax.rsqrt(ms + EPS) * g


def _const_spec(shape):
    nd = len(shape)
    return pl.BlockSpec(shape, lambda *_: (0,) * nd, pipeline_mode=pl.Buffered(1))


def _row_halves(tm):
    if tm < 2 * MIN_HALF_TILE:
        return (slice(0, tm),)
    return (slice(0, tm // 2), slice(tm // 2, tm))


def _swiglu_act(x, npre, wg_ref, wu_ref, act_ref):
    xn = _rms(x, npre).astype(bf16)
    d_ff = wg_ref.shape[1]
    for c0 in range(0, d_ff, FF_CHUNK):
        g = jnp.dot(xn, wg_ref[:, c0:c0 + FF_CHUNK], preferred_element_type=f32)
        u = jnp.dot(xn, wu_ref[:, c0:c0 + FF_CHUNK], preferred_element_type=f32)
        act_ref[:, c0:c0 + FF_CHUNK] = (g * jax.nn.sigmoid(g) * u).astype(bf16)


def _norm_rope_slab(x, g, cos, sin, lane):
    sq = x * x
    lo = jnp.sum(jnp.where(lane < HEAD_DIM, sq, 0.0), axis=-1, keepdims=True)
    hi = jnp.sum(jnp.where(lane < HEAD_DIM, 0.0, sq), axis=-1, keepdims=True)
    ms = jnp.where(lane < HEAD_DIM, lo, hi) * (1.0 / HEAD_DIM)
    xn = x * lax.rsqrt(ms + EPS) * g
    partner = jnp.where((lane % 32) < 16, pltpu.roll(xn, LANES - 16, 1), pltpu.roll(xn, 16, 1))
    return xn * cos + partner * sin


def _ffn_in_kernel(x_ref, cos_ref, sin_ref, n1pre_ref, wg_ref, wu_ref, wd_ref, n1post_ref,
                   mixpre_ref, win_ref, qn_ref, kn_ref,
                   h1_ref, qa_ref, ka_ref, vat_ref, qb_ref, kb_ref, vbt_ref, act_ref,
                   *, na_chunk):
    tm = x_ref.shape[0]
    halves = _row_halves(tm)
    _swiglu_act(x_ref[...], n1pre_ref[...], wg_ref, wu_ref, act_ref)
    ys = [jnp.dot(act_ref[r, :], wd_ref[...], preferred_element_type=f32) for r in halves]
    qscale = HEAD_DIM ** -0.5 * LOG2_E
    qn = qn_ref[...]
    kn = kn_ref[...]
    for r, y in zip(halves, ys):
        n = r.stop - r.start
        assert n % na_chunk == 0
        h1 = x_ref[r, :] + 0.5 * _rms(y, n1post_ref[...])
        h1_ref[r, :] = h1
        hn = _rms(h1, mixpre_ref[...]).astype(bf16)

        def proj(c0, width, hn=hn):
            return jnp.dot(hn, win_ref[:, c0:c0 + width], preferred_element_type=f32)

        lane = lax.broadcasted_iota(jnp.int32, (n, LANES), 1)
        cos = cos_ref[r, :]
        sin = sin_ref[r, :]
        kv = proj(3 * W_A + W_B, 2 * W_KV)
        kb_ref[r, :] = _norm_rope_slab(kv[:, :W_KV], kn, cos, sin, lane).astype(bf16)
        vbt_ref[0, :, r] = kv[:, W_KV:].T.astype(bf16)
        for s in range(0, W_B, 2 * LANES):
            qpair = proj(3 * W_A + s, 2 * LANES)
            for c0 in (0, LANES):
                qb_ref[r, s + c0:s + c0 + LANES] = (
                    _norm_rope_slab(qpair[:, c0:c0 + LANES], qn, cos, sin, lane) * qscale).astype(bf16)
        va = proj(2 * W_A, W_A)
        for j in range(n // na_chunk):
            vat_ref[r.start // na_chunk + j] = va[j * na_chunk:(j + 1) * na_chunk, :].T.astype(bf16)
        ka_ref[r, :] = proj(W_A, W_A).astype(bf16)
        qa_ref[r, :] = (proj(0, W_A) * qscale).astype(bf16)


def _ffn_in(x2, cos, sin, p, *, tm, na_chunk, gqa_chunk):
    t, d = x2.shape
    s_len = cos.shape[0]
    assert t % tm == 0 and s_len % tm == 0 and tm % na_chunk == 0
    assert gqa_chunk % tm == 0 and t % gqa_chunk == 0
    n_pos = s_len // tm
    tiles_per_chunk = gqa_chunk // tm
    d_ff = p['wg1'].shape[1]
    in_cols = p['w_in'].shape[1]
    tok = lambda w: pl.BlockSpec((tm, w), lambda i: (i, 0))
    pos = pl.BlockSpec((tm, LANES), lambda i: (i % n_pos, 0))
    out_shape = (
        jax.ShapeDtypeStruct((t, d), f32),
        jax.ShapeDtypeStruct((t, W_A), bf16),
        jax.ShapeDtypeStruct((t, W_A), bf16),
        jax.ShapeDtypeStruct((t // na_chunk, W_A, na_chunk), bf16),
        jax.ShapeDtypeStruct((t, W_B), bf16),
        jax.ShapeDtypeStruct((t, W_KV), bf16),
        jax.ShapeDtypeStruct((t // gqa_chunk, W_KV, gqa_chunk), bf16),
    )
    out_specs = (
        tok(d), tok(W_A), tok(W_A),
        pl.BlockSpec((tm // na_chunk, W_A, na_chunk), lambda i: (i, 0, 0)),
        tok(W_B), tok(W_KV),
        pl.BlockSpec((1, W_KV, tm), lambda i: (i // tiles_per_chunk, 0, i % tiles_per_chunk)),
    )
    in_specs = [
        tok(d), pos, pos,
        _const_spec((1, d)), _const_spec((d, d_ff)), _const_spec((d, d_ff)), _const_spec((d_ff, d)),
        _const_spec((1, d)), _const_spec((1, d)), _const_spec((d, in_cols)),
        _const_spec((1, LANES)), _const_spec((1, LANES)),
    ]
    return pl.pallas_call(
        functools.partial(_ffn_in_kernel, na_chunk=na_chunk),
        grid=(t // tm,),
        in_specs=in_specs,
        out_specs=out_specs,
        out_shape=out_shape,
        scratch_shapes=[pltpu.VMEM((tm, d_ff), bf16)],
        compiler_params=pltpu.CompilerParams(dimension_semantics=("arbitrary",),
                                             vmem_limit_bytes=VMEM_LIMIT),
        name="ffn_in",
    )(x2, cos, sin, p['n1pre'], p['wg1'], p['wu1'], p['wd1'], p['n1post'],
      p['mixpre'], p['w_in'], p['qn'], p['kn'])


def _na_kernel(q_ref, k0_ref, k1_ref, k2_ref, v0_ref, v1_ref, v2_ref, km_ref, vm_ref,
               strips_ref, mb_ref, g_ref, o_ref, kcat_ref, vcat_ref, s_ref, ot_ref):
    j = pl.program_id(1)
    n_blk = pl.num_programs(1)
    blk0 = jnp.clip(j - 1, 0, n_blk - 3)
    n_keys = NA_KEY_ROWS * GRID_W
    for p, (k_ref, v_ref) in enumerate(((k0_ref, v0_ref), (k1_ref, v1_ref), (k2_ref, v2_ref))):
        kcat_ref[p * NA_BLOCK:(p + 1) * NA_BLOCK, :] = k_ref[...]
        vcat_ref[:, p * NA_BLOCK:(p + 1) * NA_BLOCK] = v_ref[0]
    kcat_ref[n_keys:, :] = km_ref[...]
    vcat_ref[:, n_keys:] = vm_ref[0]

    n_strips = strips_ref.shape[2]
    strip0 = NA_ROWS * (blk0 - j) + NA_WIN_H - 1 - STRIP_D0 + 1
    strip_idx = [jnp.clip(strip0 + a, 0, n_strips - 1) for a in range(NA_KEY_ROWS)]
    lane = lax.broadcasted_iota(jnp.int32, (NA_BLOCK, LANES), 1)
    n_buf = s_ref.shape[0]

    def scores(h):
        sl = slice(LANES * (h // 2), LANES * (h // 2) + LANES)
        in_head = (lane < HEAD_DIM) if h % 2 == 0 else (lane >= HEAD_DIM)
        qs = q_ref[:, sl]
        qm = jnp.where(in_head, qs, jnp.zeros_like(qs))
        s = lax.dot_general(kcat_ref[:, sl], qm, _NT, preferred_element_type=f32)
        s = s + jnp.concatenate([strips_ref[0, h, i] for i in strip_idx] + [mb_ref[h]], axis=0)
        s_ref[h % n_buf] = s
        return jnp.max(s, axis=0, keepdims=True)

    ms =[scores(h) for h in range(n_buf - 1)]
    for h in range(NA_HEADS):
        if h + n_buf - 1 < NA_HEADS:
            ms.append(scores(h + n_buf - 1))
        hs = slice(HEAD_DIM * h, HEAD_DIM * (h + 1))
        p = jnp.exp2(s_ref[h % n_buf] - ms[h])
        l = jnp.sum(p, axis=0, keepdims=True)
        acc = jnp.dot(vcat_ref[hs, :], p.astype(bf16), preferred_element_type=f32)
        ot_ref[hs, :] = acc / l
    o_ref[...] = _rms(ot_ref[...].T, g_ref[...]).astype(bf16)


def _na_attn(qa, ka, vat, km, vmt, strips, mb, g, *, batch, n_rows):
    t = qa.shape[0]
    assert n_rows % NA_ROWS == 0
    n_blk = n_rows // NA_ROWS
    assert n_blk >= 3 and t == batch * n_blk * NA_BLOCK
    n_cat = NA_KEY_ROWS * GRID_W + META_PAD

    def kv_idx(p):
        return lambda b, j: b * n_blk + jnp.clip(j - 1, 0, n_blk - 3) + p

    k_specs = [pl.BlockSpec((NA_BLOCK, W_A), (lambda f: lambda b, j: (f(b, j), 0))(kv_idx(p))) for p in range(3)]
    v_specs = [pl.BlockSpec((1, W_A, NA_BLOCK), (lambda f: lambda b, j: (f(b, j), 0, 0))(kv_idx(p)))
               for p in range(3)]
    q_spec = pl.BlockSpec((NA_BLOCK, W_A), lambda b, j: (b * n_blk + j, 0))
    strip_spec = pl.BlockSpec((1,) + strips.shape[1:],
                              lambda b, j: (jnp.where(j == 0, 0, jnp.where(j == n_blk - 1, 2, 1)), 0, 0, 0, 0))
    return pl.pallas_call(
        _na_kernel,
        grid=(batch, n_blk),
        in_specs=[q_spec, *k_specs, *v_specs,
                  _const_spec(km.shape), _const_spec(vmt.shape), strip_spec,
                  _const_spec(mb.shape), _const_spec(g.shape)],
        out_specs=pl.BlockSpec((NA_BLOCK, W_A), lambda b, j: (b * n_blk + j, 0)),
        out_shape=jax.ShapeDtypeStruct((t, W_A), bf16),
        scratch_shapes=[pltpu.VMEM((n_cat, W_A), bf16), pltpu.VMEM((W_A, n_cat), bf16),
                        pltpu.VMEM((NA_LOOKAHEAD + 1, n_cat, NA_BLOCK), f32), pltpu.VMEM((W_A, NA_BLOCK), f32)],
        compiler_params=pltpu.CompilerParams(dimension_semantics=("arbitrary", "arbitrary"),
                                             vmem_limit_bytes=VMEM_LIMIT),
        name="na_attn",
    )(qa, ka, ka, ka, vat, vat, vat, km, vmt, strips, mb, g)


def _gqa_kernel(q_ref, q_next_ref, k_ref, vt_ref, km_ref, vmt_ref, g_ref, o_ref,
                s_ref, ot_ref, sm_ref, m_ref):
    n_buf = s_ref.shape[0]
    lookahead = n_buf // 2
    assert GQA_HEADS % n_buf == 0
    tq = q_ref.shape[0]
    lane = lax.broadcasted_iota(jnp.int32, (tq, LANES), 1)
    meta_row = lax.broadcasted_iota(jnp.int32, (META_PAD, tq), 0)

    def head_query(q_ref, h):
        n = h // GQA_GROUP
        qs = q_ref[:, LANES * (h // 2):LANES * (h // 2) + LANES].astype(f32)
        if h % 2 != n:
            qs = pltpu.roll(qs, HEAD_DIM, 1)
        on_kv = (lane < HEAD_DIM) if n == 0 else (lane >= HEAD_DIM)
        return jnp.where(on_kv, qs, 0.0).astype(bf16)

    def meta_scores(qm):
        sm = lax.dot_general(km_ref[...], qm, _NT, preferred_element_type=f32)
        return jnp.where(meta_row < N_META, sm, NEG)

    def scores(h):
        qm = head_query(q_ref, h) if h < GQA_HEADS else head_query(q_next_ref, 0)
        sm = meta_scores(qm)
        s = lax.dot_general(k_ref[...], qm, _NT, preferred_element_type=f32)
        s_ref[h % n_buf] = s
        return sm, jnp.maximum(jnp.max(sm, axis=0, keepdims=True), jnp.max(s, axis=0, keepdims=True))

    @pl.when(pl.program_id(1) == 0)
    def _():
        sm_ref[...], m_ref[...] = scores(0)

    pending = {0: (sm_ref[...], m_ref[...])}
    issued = 1
    for h in range(GQA_HEADS):
        while issued <= min(h + lookahead, GQA_HEADS):
            pending[issued] = scores(issued)
            issued += 1
        sm, m = pending.pop(h)
        vs = slice(HEAD_DIM * (h // GQA_GROUP), HEAD_DIM * (h // GQA_GROUP + 1))
        pm = jnp.exp2(sm - m)
        p = jnp.exp2(s_ref[h % n_buf] - m)
        l = jnp.sum(pm, axis=0, keepdims=True) + jnp.sum(p, axis=0, keepdims=True)
        acc = (jnp.dot(vmt_ref[0, vs, :], pm.astype(bf16), preferred_element_type=f32)
               + jnp.dot(vt_ref[0, vs, :], p.astype(bf16), preferred_element_type=f32))
        ot_ref[HEAD_DIM * h:HEAD_DIM * (h + 1), :] = acc / l
    sm_ref[...], m_ref[...] = pending.pop(GQA_HEADS)
    o_ref[...] = _rms(ot_ref[...].T, g_ref[...]).astype(bf16)


def _gqa_attn(qb, kb, vbt, km, vmt, g, *, batch, seq):
    t = qb.shape[0]
    tq = GQA_Q_TILE
    assert t == batch * seq and seq % tq == 0 and vbt.shape == (batch, W_KV, seq)
    n_q = seq // tq
    n_buf = max(n for n in GQA_SCORE_BUFFERS if n * seq * tq * 4 <= GQA_SCORE_BYTES or n == min(GQA_SCORE_BUFFERS))
    return pl.pallas_call(
        _gqa_kernel,
        grid=(batch, n_q),
        in_specs=[pl.BlockSpec((tq, W_B), lambda b, i: (b * n_q + i, 0)),
                  pl.BlockSpec((tq, W_B), lambda b, i: (b * n_q + jnp.minimum(i + 1, n_q - 1), 0)),
                  pl.BlockSpec((seq, W_KV), lambda b, i: (b, 0), pipeline_mode=pl.Buffered(1)),
                  pl.BlockSpec((1, W_KV, seq), lambda b, i: (b, 0, 0), pipeline_mode=pl.Buffered(1)),
                  _const_spec(km.shape), _const_spec(vmt.shape), _const_spec(g.shape)],
        out_specs=pl.BlockSpec((tq, W_B), lambda b, i: (b * n_q + i, 0)),
        out_shape=jax.ShapeDtypeStruct((t, W_B), bf16),
        scratch_shapes=[pltpu.VMEM((n_buf, seq, tq), f32), pltpu.VMEM((W_B, tq), f32),
                        pltpu.VMEM((META_PAD, tq), f32), pltpu.VMEM((1, tq), f32)],
        compiler_params=pltpu.CompilerParams(dimension_semantics=("arbitrary", "arbitrary"),
                                             vmem_limit_bytes=VMEM_LIMIT),
        name="gqa_attn",
    )(qb, qb, kb, vbt, km, vmt, g)


def _out_ffn_kernel(h1_ref, ya_ref, yb_ref, woa_ref, wob_ref, mixpost_ref,
                    n2pre_ref, wg_ref, wu_ref, wd_ref, n2post_ref, fin_ref, o_ref, act_ref, h2_ref):
    halves = _row_halves(h1_ref.shape[0])
    ys = [jnp.dot(ya_ref[r, :], woa_ref[...], preferred_element_type=f32)
          + jnp.dot(yb_ref[r, :], wob_ref[...], preferred_element_type=f32) for r in halves]
    for r, y in zip(halves, ys):
        h2_ref[r, :] = h1_ref[r, :] + _rms(y, mixpost_ref[...])
    for r in halves:
        _swiglu_act(h2_ref[r, :], n2pre_ref[...], wg_ref, wu_ref, act_ref.at[r, :])
    ys = [jnp.dot(act_ref[r, :], wd_ref[...], preferred_element_type=f32) for r in halves]
    for r, y in zip(halves, ys):
        h3 = h2_ref[r, :] + 0.5 * _rms(y, n2post_ref[...])
        o_ref[r, :] = _rms(h3, fin_ref[...])


def _out_ffn(h1, ya, yb, p, *, tm):
    t, d = h1.shape
    d_ff = p['wg2'].shape[1]
    tok = lambda w: pl.BlockSpec((tm, w), lambda i: (i, 0))
    return pl.pallas_call(
        _out_ffn_kernel,
        grid=(t // tm,),
        in_specs=[tok(d), tok(W_A), tok(W_B),
                  _const_spec((W_A, d)), _const_spec((W_B, d)), _const_spec((1, d)),
                  _const_spec((1, d)), _const_spec((d, d_ff)), _const_spec((d, d_ff)),
                  _const_spec((d_ff, d)), _const_spec((1, d)), _const_spec((1, d))],
        out_specs=tok(d),
        out_shape=jax.ShapeDtypeStruct((t, d), f32),
        scratch_shapes=[pltpu.VMEM((tm, d_ff), bf16), pltpu.VMEM((tm, d), f32)],
        compiler_params=pltpu.CompilerParams(dimension_semantics=("arbitrary",),
                                             vmem_limit_bytes=VMEM_LIMIT),
        name="out_ffn",
    )(h1, ya, yb, p['woa'], p['wob'], p['mixpost'], p['n2pre'], p['wg2'], p['wu2'], p['wd2'],
      p['n2post'], p['fin'])


def _rope_tables(pos_row, pos_col):
    inv_freq = jnp.asarray(ROPE_THETA ** (-np.arange(0, ROPE_AXIS_DIM, 2) / ROPE_AXIS_DIM), f32)
    ang_r = pos_row[:, None] * inv_freq[None, :]
    ang_c = pos_col[:, None] * inv_freq[None, :]
    cos = jnp.concatenate([jnp.cos(ang_r)] * 2 + [jnp.cos(ang_c)] * 2, axis=-1)
    sin = jnp.concatenate([-jnp.sin(ang_r), jnp.sin(ang_r), -jnp.sin(ang_c), jnp.sin(ang_c)], axis=-1)
    return jnp.tile(cos, (1, 2)), jnp.tile(sin, (1, 2))


def _bias_strips(rel_bias):
    n_h, n_d, n_k = rel_bias.shape
    assert n_d == 2 * NA_WIN_H - 1 and n_k == 2 * NA_WIN_W - 1
    period = 2 * GRID_W - 1
    lead = GRID_W - NA_WIN_W
    z = jnp.concatenate([jnp.zeros((n_h, n_d, lead), f32), rel_bias.astype(f32)[..., ::-1],
                         jnp.zeros((n_h, n_d, period - lead - n_k), f32)], axis=-1)
    hank = jnp.tile(z, (1, 1, GRID_W + 1))[..., :GRID_W * (period + 1)]
    toep = hank.reshape(n_h, n_d, GRID_W, period + 1)[..., ::-1, :GRID_W]
    c = np.arange(GRID_W)[:, None]
    q = np.arange(GRID_W)[None, :]
    col_start = np.clip(q - NA_WIN_W // 2, 0, GRID_W - NA_WIN_W)
    toep = jnp.where((c >= col_start) & (c < col_start + NA_WIN_W), toep * LOG2_E, NEG)
    n_strips = n_d - STRIP_D0
    strips = jnp.concatenate([toep[:, STRIP_D0 - b:STRIP_D0 - b + n_strips] for b in range(NA_ROWS)], axis=-1)
    b = np.arange(NA_ROWS)[None, :]
    rel = np.arange(n_strips)[:, None] + STRIP_D0 - b - (NA_WIN_H - 1)
    half = NA_WIN_H // 2
    row_ok = np.stack([(rel >= -b) & (rel < NA_WIN_H - b),
                       (rel >= -half) & (rel < half),
                       (rel >= -half - b) & (rel < half - b)])
    row_ok = np.repeat(row_ok, GRID_W, axis=-1)[:, None, :, None, :]
    strips = jnp.where(row_ok, strips[None], NEG)
    return jnp.pad(strips, ((0, 0), (0, 0), (1, 1), (0, 0), (0, 0)), constant_values=NEG)


def kernel(x_prompt, x_sample, meta_tokens, ffn1_norm_pre, ffn1_w_gate, ffn1_w_up, ffn1_w_down, ffn1_norm_post, mix_norm_pre, w_in, na_rel_bias, na_meta_bias, gqa_q_norm, gqa_k_norm, grp_norm_a, grp_norm_b, w_out, mix_norm_post, ffn2_norm_pre, ffn2_w_gate, ffn2_w_up, ffn2_w_down, ffn2_norm_post, final_norm):
    assert ffn1_w_gate.shape[0] == 1, "meta keys/values are shared across the batch only for depth 1"
    d = x_prompt.shape[-1]
    row = lambda v: v.reshape(1, -1).astype(f32)
    p = dict(
        n1pre=row(ffn1_norm_pre[0]), wg1=ffn1_w_gate[0].astype(bf16), wu1=ffn1_w_up[0].astype(bf16),
        wd1=ffn1_w_down[0].astype(bf16), n1post=row(ffn1_norm_post[0]),
        mixpre=row(mix_norm_pre[0]), w_in=w_in[0].astype(bf16),
        qn=row(jnp.tile(gqa_q_norm[0], 2)), kn=row(jnp.tile(gqa_k_norm[0], 2)),
        woa=w_out[0, :W_A].astype(bf16), wob=w_out[0, W_A:].astype(bf16), mixpost=row(mix_norm_post[0]),
        n2pre=row(ffn2_norm_pre[0]), wg2=ffn2_w_gate[0].astype(bf16), wu2=ffn2_w_up[0].astype(bf16),
        wd2=ffn2_w_down[0].astype(bf16), n2post=row(ffn2_norm_post[0]), fin=row(final_norm),
    )
    ga = row(grp_norm_a[0])
    gb = row(grp_norm_b[0])
    strips = _bias_strips(na_rel_bias[0])
    mb = jnp.where(np.arange(META_PAD)[None, :, None] < N_META,
                   jnp.pad(na_meta_bias[0].astype(f32) * LOG2_E, ((0, 0), (0, META_PAD - N_META)))[:, :, None], NEG)
    mb = jnp.broadcast_to(mb, (NA_HEADS, META_PAD, NA_BLOCK))

    xm = jnp.pad(meta_tokens.astype(f32), ((0, META_PAD - N_META), (0, 0)))
    mcos, msin = _rope_tables(jnp.pad(-jnp.ones((N_META,), f32), (0, META_PAD - N_META)),
                              jnp.pad(jnp.arange(N_META, dtype=f32), (0, META_PAD - N_META)))
    _, _, kam, vamt, _, kbm, vbmt = _ffn_in(xm, mcos, msin, p, tm=META_PAD, na_chunk=META_PAD,
                                            gqa_chunk=META_PAD)

    def run(x):
        batch, seq, _ = x.shape
        n_rows = seq // GRID_W
        tpos = jnp.arange(seq)
        cos, sin = _rope_tables((tpos // GRID_W).astype(f32), (tpos % GRID_W).astype(f32))
        h1, qa, ka, vat, qb, kb, vbt = _ffn_in(x.reshape(batch * seq, d), cos, sin, p, tm=TOKEN_TILE,
                                               na_chunk=NA_BLOCK, gqa_chunk=seq)
        ya = _na_attn(qa, ka, vat, kam, vamt, strips, mb, ga, batch=batch, n_rows=n_rows)
        yb = _gqa_attn(qb, kb, vbt, kbm, vbmt, gb, batch=batch, seq=seq)
        return _out_ffn(h1, ya, yb, p, tm=TOKEN_TILE).reshape(batch, seq, d)

    return (run(x_prompt), run(x_sample))
```
